```python
import math
import jax, jax.numpy as jnp
from jax import lax
import numpy as np

D_MODEL = 1024
BATCH = 16
SEQ = 4096
DEPTH = 1
DEC_BATCH = 8
DEC_SEQ = 32
PAST_LEN = 2048

CHUNK = 64
Q_BLOCK = 128
MLA_HEADS = 8
Q_LORA = 256
KV_LORA = 128
QK_NOPE = 64
QK_ROPE = 32
MLA_V = 64
ROPE_BASE = 10000.0
DIFF_HEADS = 4
DIFF_HD = 64
DIFF_W = DIFF_HEADS * 2 * DIFF_HD
N_BRANCHES = 2
SPLIT_POINTS = (Q_LORA, Q_LORA + KV_LORA, Q_LORA + KV_LORA + QK_ROPE,
                Q_LORA + KV_LORA + QK_ROPE + DIFF_W, Q_LORA + KV_LORA + QK_ROPE + 2 * DIFF_W,
                Q_LORA + KV_LORA + QK_ROPE + 3 * DIFF_W)
IN_WIDTH = Q_LORA + KV_LORA + QK_ROPE + 3 * DIFF_W + N_BRANCHES * D_MODEL
N_EXPERTS = 256
N_GROUPS = 8
TOPK_GROUPS = 4
TOP_K = 8
D_EXPERT = 256
D_SHARED = 256
ROUTED_SCALE = 2.5
MOE_BLOCK = 128
EPS = 1e-6
ALPHA = (2 * DEPTH) ** 0.25
BETA = (8 * DEPTH) ** -0.25

kernel_name = "mla_diffattn_gated_moe_streaming_encoder_step"


def rms_norm(x, g):
    xf = x.astype(jnp.float32)
    y = xf * lax.rsqrt(jnp.mean(xf * xf, axis=-1, keepdims=True) + EPS)
    return (y * g.astype(jnp.float32)).astype(x.dtype)


def layer_norm(x, g, b):
    xf = x.astype(jnp.float32)
    mu = jnp.mean(xf, axis=-1, keepdims=True)
    var = jnp.mean(jnp.square(xf - mu), axis=-1, keepdims=True)
    y = (xf - mu) * lax.rsqrt(var + EPS) * g.astype(jnp.float32) + b.astype(jnp.float32)
    return y.astype(x.dtype)


def rope(x, pos):
    half = QK_ROPE // 2
    inv = ROPE_BASE ** (-jnp.arange(half, dtype=jnp.float32) / half)
    ang = pos.astype(jnp.float32)[:, None] * inv
    ang = ang.reshape(ang.shape[:1] + (1,) * (x.ndim - 3) + (half,))
    cos, sin = jnp.cos(ang), jnp.sin(ang)
    xf = x.astype(jnp.float32)
    x1, x2 = xf[..., :half], xf[..., half:]
    return jnp.concatenate([x1 * cos - x2 * sin, x1 * sin + x2 * cos], axis=-1).astype(x.dtype)


def chunk_causal(q_pos, k_pos):
    return (k_pos[None, :] // CHUNK) <= (q_pos[:, None] // CHUNK)


def over_query_blocks(fn, q_pos, *qs):
    s = q_pos.shape[0]
    qb = Q_BLOCK if s % Q_BLOCK == 0 else s
    nb = s // qb
    def split(a):
        return jnp.moveaxis(a.reshape((a.shape[0], nb, qb) + a.shape[2:]), 1, 0)
    out = lax.map(lambda args: fn(*args), (q_pos.reshape(nb, qb),) + tuple(split(q) for q in qs))
    out = jnp.moveaxis(out, 0, 1)
    return out.reshape((out.shape[0], s) + out.shape[3:])


def mla_attention(q_lat, q_rope, lat, k_rope, q_pos, k_pos, w_uv):
    scale = (QK_NOPE + QK_ROPE) ** -0.5
    def block(qp, ql, qr):
        s = (jnp.einsum('bqhc,bkc->bhqk', ql, lat) +
             jnp.einsum('bqhr,bkr->bhqk', qr, k_rope)).astype(jnp.float32) * scale
        s = jnp.where(chunk_causal(qp, k_pos), s, -jnp.inf)
        p = jax.nn.softmax(s, axis=-1).astype(lat.dtype)
        o_lat = jnp.einsum('bhqk,bkc->bqhc', p, lat)
        return jnp.einsum('bqhc,chv->bqhv', o_lat, w_uv)
    return over_query_blocks(block, q_pos, q_lat, q_rope)


def diff_attention(q, k, v, q_pos, k_pos, lam):
    slopes = 2.0 ** (-8.0 * jnp.arange(1, DIFF_HEADS + 1, dtype=jnp.float32) / DIFF_HEADS)
    scale = DIFF_HD ** -0.5
    def block(qp, qb):
        s = jnp.einsum('bqhid,bkhid->bhiqk', qb, k).astype(jnp.float32) * scale
        dist = jnp.abs(qp[:, None] - k_pos[None, :]).astype(jnp.float32)
        s = s - slopes[:, None, None, None] * dist
        s = jnp.where(chunk_causal(qp, k_pos), s, -jnp.inf)
        p = jax.nn.softmax(s, axis=-1)
        a = (p[:, :, 0] - lam * p[:, :, 1]).astype(v.dtype)
        return jnp.einsum('bhqk,bkhe->bqhe', a, v)
    return over_query_blocks(block, q_pos, q)


def route(xf, w_router, router_bias):
    n = xf.shape[0]
    scores = jax.nn.sigmoid((xf @ w_router).astype(jnp.float32))
    choice = scores + router_bias.astype(jnp.float32)
    grp = choice.reshape(n, N_GROUPS, N_EXPERTS // N_GROUPS)
    grp_score = lax.top_k(grp, 2)[0].sum(-1)
    top_g = lax.top_k(grp_score, TOPK_GROUPS)[1]
    gmask = (top_g[:, :, None] == jnp.arange(N_GROUPS)[None, None, :]).any(axis=1)
    emask = jnp.repeat(gmask, N_EXPERTS // N_GROUPS, axis=1)
    idx = lax.top_k(jnp.where(emask, choice, -jnp.inf), TOP_K)[1]
    wts = jnp.take_along_axis(scores, idx, axis=1)
    wts = wts / jnp.sum(wts, axis=-1, keepdims=True) * ROUTED_SCALE
    return idx, wts


def routed_experts(xf, idx, wts, w_gate, w_up, w_down):
    n, d = xf.shape
    a = n * TOP_K
    flat_e = idx.reshape(a)
    flat_w = wts.reshape(a)
    order = jnp.argsort(flat_e)
    e_sorted = flat_e[order]
    counts = jnp.zeros((N_EXPERTS,), jnp.int32).at[flat_e].add(1)
    start = jnp.cumsum(counts) - counts
    padded = (counts + MOE_BLOCK - 1) // MOE_BLOCK * MOE_BLOCK
    pad_end = jnp.cumsum(padded)
    pad_start = pad_end - padded
    dest = pad_start[e_sorted] + jnp.arange(a, dtype=jnp.int32) - start[e_sorted]
    n_blocks = -(-a // MOE_BLOCK) + N_EXPERTS
    rows = n_blocks * MOE_BLOCK
    row_tok = jnp.full((rows,), n, jnp.int32).at[dest].set((order // TOP_K).astype(jnp.int32))
    row_w = jnp.zeros((rows,), xf.dtype).at[dest].set(flat_w[order])
    block_e = jnp.minimum(jnp.searchsorted(pad_end, jnp.arange(n_blocks, dtype=jnp.int32) * MOE_BLOCK,
                                           side='right'), N_EXPERTS - 1)
    x_ext = jnp.concatenate([xf, jnp.zeros((1, d), xf.dtype)], axis=0)
    def body(acc, blk):
        tok, w, e = blk
        xb = x_ext[tok]
        h = jax.nn.silu(xb @ w_gate[e]) * (xb @ w_up[e])
        return acc.at[tok].add((h @ w_down[e]) * w[:, None]), None
    acc, _ = lax.scan(body, jnp.zeros((n + 1, d), xf.dtype),
                      (row_tok.reshape(n_blocks, MOE_BLOCK), row_w.reshape(n_blocks, MOE_BLOCK), block_e))
    return acc[:n]


def moe_ffn(x, w):
    b, s, d = x.shape
    xf = x.reshape(b * s, d)
    idx, wts = route(xf, w['w_router'], w['router_bias'])
    routed = routed_experts(xf, idx, wts.astype(x.dtype), w['w_exp_gate'], w['w_exp_up'], w['w_exp_down'])
    shared = (jax.nn.silu(xf @ w['w_sh_gate']) * (xf @ w['w_sh_up'])) @ w['w_sh_down']
    return (routed + shared).reshape(b, s, d)


def encoder_layer(x, past, layer_idx, w):
    b, s, _ = x.shape
    p_len = 0 if past is None else past[0].shape[1]
    q_pos = p_len + jnp.arange(s, dtype=jnp.int32)
    k_pos = jnp.arange(p_len + s, dtype=jnp.int32)
    h = x @ w['w_in']
    c_q, c_kv, kr, dq, dk, dv, gates = jnp.split(h, SPLIT_POINTS, axis=-1)
    c_q = rms_norm(c_q, w['g_q_norm'])
    q = jnp.einsum('bsc,chd->bshd', c_q, w['w_uq'])
    q_rope = rope(q[..., QK_NOPE:], q_pos)
    q_lat = jnp.einsum('bshd,chd->bshc', q[..., :QK_NOPE], w['w_uk'])
    c_kv = rms_norm(c_kv, w['g_kv_norm'])
    kr = rope(kr, q_pos)
    dq = dq.reshape(b, s, DIFF_HEADS, 2, DIFF_HD)
    dk = dk.reshape(b, s, DIFF_HEADS, 2, DIFF_HD)
    dv = dv.reshape(b, s, DIFF_HEADS, 2 * DIFF_HD)
    if past is None:
        lat_all, kr_all, dk_all, dv_all = c_kv, kr, dk, dv
    else:
        lat_all = jnp.concatenate([past[0], c_kv], axis=1)
        kr_all = jnp.concatenate([past[1], kr], axis=1)
        dk_all = jnp.concatenate([past[2], dk], axis=1)
        dv_all = jnp.concatenate([past[3], dv], axis=1)
    o_a = mla_attention(q_lat, q_rope, lat_all, kr_all, q_pos, k_pos, w['w_uv'])
    branch_a = o_a.reshape(b, s, MLA_HEADS * MLA_V) @ w['w_o_mla']
    lam_init = 0.8 - 0.6 * math.exp(-0.3 * layer_idx)
    lam = (jnp.exp(jnp.sum(w['lambda_q1'].astype(jnp.float32) * w['lambda_k1'].astype(jnp.float32))) -
           jnp.exp(jnp.sum(w['lambda_q2'].astype(jnp.float32) * w['lambda_k2'].astype(jnp.float32))) + lam_init)
    o_b = diff_attention(dq, dk_all, dv_all, q_pos, k_pos, lam)
    o_b = rms_norm(o_b, w['g_subln']) * (1.0 - lam_init)
    branch_b = o_b.reshape(b, s, DIFF_W) @ w['w_o_diff']
    g = jax.nn.sigmoid(gates + w['b_gate'])
    g_a, g_b = g[..., :D_MODEL], g[..., D_MODEL:]
    mix = (g_a * branch_a + g_b * branch_b) @ w['w_out']
    x1 = layer_norm(ALPHA * x + mix, w['g_ln1'], w['b_ln1'])
    x2 = layer_norm(ALPHA * x1 + moe_ffn(x1, w), w['g_ln2'], w['b_ln2'])
    return x2, (c_kv, kr, dk, dv)


def setup_inputs(seed: int = 0) -> dict:
    key = jax.random.key(seed)
    ks = iter(jax.random.split(key, 48))
    def nrm(shape, scale=1.0):
        return jax.random.normal(next(ks), shape, jnp.float32) * scale
    def gain(n):
        return 1.0 + nrm((DEPTH, n), 0.02)
    L = DEPTH
    return {
        "x_prompt": nrm((BATCH, SEQ, D_MODEL)),
        "x_sample": nrm((DEC_BATCH, DEC_SEQ, D_MODEL)),
        "cache_mla_latent": nrm((L, DEC_BATCH, PAST_LEN, KV_LORA)),
        "cache_mla_krope": nrm((L, DEC_BATCH, PAST_LEN, QK_ROPE)),
        "cache_diff_k": nrm((L, DEC_BATCH, PAST_LEN, DIFF_HEADS, 2, DIFF_HD)),
        "cache_diff_v": nrm((L, DEC_BATCH, PAST_LEN, DIFF_HEADS, 2 * DIFF_HD)),
        "w_in": nrm((L, D_MODEL, IN_WIDTH), D_MODEL ** -0.5),
        "b_gate": nrm((L, N_BRANCHES * D_MODEL), 0.02),
        "g_q_norm": gain(Q_LORA),
        "w_uq": nrm((L, Q_LORA, MLA_HEADS, QK_NOPE + QK_ROPE), Q_LORA ** -0.5),
        "w_uk": nrm((L, KV_LORA, MLA_HEADS, QK_NOPE), KV_LORA ** -0.5),
        "g_kv_norm": gain(KV_LORA),
        "w_uv": nrm((L, KV_LORA, MLA_HEADS, MLA_V), KV_LORA ** -0.5),
        "w_o_mla": nrm((L, MLA_HEADS * MLA_V, D_MODEL), BETA * (MLA_HEADS * MLA_V) ** -0.5),
        "lambda_q1": nrm((L, DIFF_HD), 0.1),
        "lambda_k1": nrm((L, DIFF_HD), 0.1),
        "lambda_q2": nrm((L, DIFF_HD), 0.1),
        "lambda_k2": nrm((L, DIFF_HD), 0.1),
        "g_subln": gain(2 * DIFF_HD),
        "w_o_diff": nrm((L, DIFF_W, D_MODEL), BETA * DIFF_W ** -0.5),
        "w_out": nrm((L, D_MODEL, D_MODEL), BETA * D_MODEL ** -0.5),
        "g_ln1": gain(D_MODEL),
        "b_ln1": nrm((L, D_MODEL), 0.02),
        "w_router": nrm((L, D_MODEL, N_EXPERTS), D_MODEL ** -0.5),
        "router_bias": nrm((L, N_EXPERTS), 0.01),
        "w_exp_gate": nrm((L, N_EXPERTS, D_MODEL, D_EXPERT), D_MODEL ** -0.5),
        "w_exp_up": nrm((L, N_EXPERTS, D_MODEL, D_EXPERT), D_MODEL ** -0.5),
        "w_exp_down": nrm((L, N_EXPERTS, D_EXPERT, D_MODEL), BETA * D_EXPERT ** -0.5),
        "w_sh_gate": nrm((L, D_MODEL, D_SHARED), D_MODEL ** -0.5),
        "w_sh_up": nrm((L, D_MODEL, D_SHARED), D_MODEL ** -0.5),
        "w_sh_down": nrm((L, D_SHARED, D_MODEL), BETA * D_SHARED ** -0.5),
        "g_ln2": gain(D_MODEL),
        "b_ln2": nrm((L, D_MODEL), 0.02),
    }


def reference(x_prompt, x_sample, cache_mla_latent, cache_mla_krope, cache_diff_k, cache_diff_v,
              w_in, b_gate, g_q_norm, w_uq, w_uk, g_kv_norm, w_uv, w_o_mla,
              lambda_q1, lambda_k1, lambda_q2, lambda_k2, g_subln, w_o_diff, w_out, g_ln1, b_ln1,
              w_router, router_bias, w_exp_gate, w_exp_up, w_exp_down, w_sh_gate, w_sh_up, w_sh_down,
              g_ln2, b_ln2):
    y_prompt, y_sample = x_prompt, x_sample
    st_prompt, st_sample = [], []
    for l in range(DEPTH):
        w = dict(w_in=w_in[l], b_gate=b_gate[l], g_q_norm=g_q_norm[l], w_uq=w_uq[l], w_uk=w_uk[l],
                 g_kv_norm=g_kv_norm[l], w_uv=w_uv[l], w_o_mla=w_o_mla[l],
                 lambda_q1=lambda_q1[l], lambda_k1=lambda_k1[l], lambda_q2=lambda_q2[l], lambda_k2=lambda_k2[l],
                 g_subln=g_subln[l], w_o_diff=w_o_diff[l], w_out=w_out[l], g_ln1=g_ln1[l], b_ln1=b_ln1[l],
                 w_router=w_router[l], router_bias=router_bias[l], w_exp_gate=w_exp_gate[l],
                 w_exp_up=w_exp_up[l], w_exp_down=w_exp_down[l], w_sh_gate=w_sh_gate[l],
                 w_sh_up=w_sh_up[l], w_sh_down=w_sh_down[l], g_ln2=g_ln2[l], b_ln2=b_ln2[l])
        y_prompt, st_p = encoder_layer(y_prompt, None, l, w)
        past = (cache_mla_latent[l], cache_mla_krope[l], cache_diff_k[l], cache_diff_v[l])
        y_sample, st_s = encoder_layer(y_sample, past, l, w)
        st_prompt.append(st_p)
        st_sample.append(st_s)
    new_mla_latent_prompt = jnp.stack([st[0] for st in st_prompt])
    new_mla_krope_prompt = jnp.stack([st[1] for st in st_prompt])
    new_diff_k_prompt = jnp.stack([st[2] for st in st_prompt])
    new_diff_v_prompt = jnp.stack([st[3] for st in st_prompt])
    new_mla_latent_sample = jnp.stack([st[0] for st in st_sample])
    new_mla_krope_sample = jnp.stack([st[1] for st in st_sample])
    new_diff_k_sample = jnp.stack([st[2] for st in st_sample])
    new_diff_v_sample = jnp.stack([st[3] for st in st_sample])
    return (y_prompt, y_sample, new_mla_latent_prompt, new_mla_krope_prompt, new_diff_k_prompt,
            new_diff_v_prompt, new_mla_latent_sample, new_mla_krope_sample, new_diff_k_sample,
            new_diff_v_sample)
```

```python
import functools
import math

import jax
import jax.numpy as jnp
from jax import lax
from jax.experimental import pallas as pl
from jax.experimental.pallas import tpu as pltpu

F32 = jnp.float32
BF16 = jnp.bfloat16
I32 = jnp.int32

CHUNK = 64
ROPE_BASE = 10000.0
EPS = 1e-6
N_GROUPS = 8
TOPK_GROUPS = 4
TOP_K = 8
ROUTED_SCALE = 2.5
NEG = -1e30

LANES = 128
TOK_TILE = 256
PROJ_TILE = 512
ATT_BLOCK = 256
EXP_BLOCK = 512
VMEM_LIMIT = 56 * 1024 * 1024


def _cparams(sem):
    return pltpu.CompilerParams(dimension_semantics=sem, vmem_limit_bytes=VMEM_LIMIT)


def _full(shape):
    nd = len(shape)
    return pl.BlockSpec(shape, lambda *_: (0,) * nd)


def _proj_kernel(x_ref, w_ref, gq_ref, gkv_ref, rope_ref,
                 cq_ref, lat_ref, kr_ref, kcat_ref, dq_ref, dk_ref, dkb_ref, dv_ref, dvb_ref,
                 *, q_lora, kv_lora, diff_w, qk_rope):
    x = x_ref[...].astype(BF16)
    h = jnp.dot(x, w_ref[...], preferred_element_type=F32)
    o = 0
    cq = h[:, o:o + q_lora]
    o += q_lora
    cq = cq * lax.rsqrt(jnp.mean(cq * cq, axis=-1, keepdims=True) + EPS) * gq_ref[...]
    cq_ref[...] = cq.astype(BF16)
    ckv = h[:, o:o + kv_lora]
    o += kv_lora
    lat = ckv * lax.rsqrt(jnp.mean(ckv * ckv, axis=-1, keepdims=True) + EPS) * gkv_ref[...]
    lat_ref[...] = lat
    dq_ref[...] = h[:, o:o + diff_w].astype(BF16)
    o += diff_w
    dk = h[:, o:o + diff_w]
    o += diff_w
    dk_ref[...] = dk
    dkb_ref[...] = dk.astype(BF16)
    dv = h[:, o:o + diff_w]
    o += diff_w
    dv_ref[...] = dv
    dvb_ref[...] = dv.astype(BF16)
    krr = h[:, o:o + LANES] * rope_ref[:, 0:LANES] + h[:, o + LANES:o + 2 * LANES] * rope_ref[:, LANES:2 * LANES]
    kr_ref[...] = krr[:, 0:qk_rope]
    kcat_ref[...] = jnp.concatenate([lat, krr], axis=1).astype(BF16)


def _proj(x, w1, gq, gkv, rope_k, *, tile, dims):
    n, d = x.shape
    q_lora, kv_lora, diff_w, qk_rope = dims
    wcols = w1.shape[1]
    nrope = rope_k.shape[0] // tile
    row = lambda c: pl.BlockSpec((tile, c), lambda i: (i, 0))
    outs = [
        jax.ShapeDtypeStruct((n, q_lora), BF16),
        jax.ShapeDtypeStruct((n, kv_lora), F32),
        jax.ShapeDtypeStruct((n, qk_rope), F32),
        jax.ShapeDtypeStruct((n, 2 * LANES), BF16),
        jax.ShapeDtypeStruct((n, diff_w), BF16),
        jax.ShapeDtypeStruct((n, diff_w), F32),
        jax.ShapeDtypeStruct((n, diff_w), BF16),
        jax.ShapeDtypeStruct((n, diff_w), F32),
        jax.ShapeDtypeStruct((n, diff_w), BF16),
    ]
    return pl.pallas_call(
        functools.partial(_proj_kernel, q_lora=q_lora, kv_lora=kv_lora, diff_w=diff_w, qk_rope=qk_rope),
        grid=(n // tile,),
        in_specs=[row(d), _full((d, wcols)), _full((1, q_lora)), _full((1, kv_lora)),
                  pl.BlockSpec((tile, 2 * LANES), lambda i: (i % nrope, 0))],
        out_specs=[row(q_lora), row(kv_lora), row(qk_rope), row(2 * LANES), row(diff_w), row(diff_w),
                   row(diff_w), row(diff_w), row(diff_w)],
        out_shape=outs,
        compiler_params=_cparams(("arbitrary",)),
        name="proj",
    )(x, w1, gq, gkv, rope_k)


def _softmax_step(s, v, m_scr, l_scr, acc_scr, r0, rows, kb):
    m_prev = m_scr[r0:r0 + rows, :]
    l_prev = l_scr[r0:r0 + rows, :]
    m_cur = jnp.max(s, axis=1, keepdims=True)
    m_next = jnp.maximum(m_prev, m_cur)
    p = jnp.exp(s - jnp.concatenate([m_next] * (kb // LANES), axis=1))
    alpha = jnp.exp(m_prev - m_next)
    l_scr[r0:r0 + rows, :] = alpha * l_prev + jnp.sum(p, axis=1, keepdims=True)
    m_scr[r0:r0 + rows, :] = m_next
    pv = jnp.dot(p.astype(BF16), v, preferred_element_type=F32)
    acc_scr[r0:r0 + rows, :] = acc_scr[r0:r0 + rows, :] * alpha + pv


def _visible(qb, kb, q0, k0, n_valid):
    qpos = q0 + lax.broadcasted_iota(I32, (qb, kb), 0)
    kpos = k0 + lax.broadcasted_iota(I32, (qb, kb), 1)
    sh = CHUNK.bit_length() - 1
    vis = (lax.shift_right_logical(kpos, sh) <= lax.shift_right_logical(qpos, sh)) & (kpos < n_valid)
    return vis, qpos, kpos


def _mla_kernel(cq_ref, ropeq_ref, kcat_ref, wqn_ref, wqr_ref, wqs_ref, wuk_ref, wuv_ref, o_ref,
                qcat_scr, m_scr, l_scr, acc_scr, *, qb, kb, nh, qk_rope, past, n_valid, scale, causal):
    i = pl.program_id(1)
    rows = nh * qb
    cq = cq_ref[0]
    qn = jnp.dot(cq, wqn_ref[...], preferred_element_type=F32).astype(BF16)
    qr = jnp.dot(cq, wqr_ref[...], preferred_element_type=F32)
    qs = jnp.dot(cq, wqs_ref[...], preferred_element_type=F32)
    rw = nh * qk_rope
    rot = qr * ropeq_ref[:, 0:rw] + qs * ropeq_ref[:, rw:2 * rw]
    lane = lax.broadcasted_iota(I32, (qb, LANES), 1)
    per = LANES // qk_rope
    for j in range(nh // 2):
        ql = jnp.dot(qn[:, LANES * j:LANES * (j + 1)], wuk_ref[j], preferred_element_type=F32)
        for u in range(2):
            hh = 2 * j + u
            rblk = rot[:, LANES * (hh // per):LANES * (hh // per + 1)]
            lo = qk_rope * (hh % per)
            part2 = jnp.where((lane >= lo) & (lane < lo + qk_rope), rblk, 0.0)
            qcat_scr[hh * qb:(hh + 1) * qb, :] = jnp.concatenate(
                [ql[:, LANES * u:LANES * (u + 1)], part2], axis=1).astype(BF16)
    m_scr[...] = jnp.full(m_scr.shape, NEG, F32)
    l_scr[...] = jnp.zeros(l_scr.shape, F32)
    acc_scr[...] = jnp.zeros(acc_scr.shape, F32)

    def block(j, masked):
        k0 = pl.multiple_of(j * kb, kb)
        kblk = kcat_ref[0, pl.ds(k0, kb), :]
        s = lax.dot_general(qcat_scr[...], kblk, (((1,), (1,)), ((), ())),
                            preferred_element_type=F32) * scale
        if masked:
            vis, _, _ = _visible(qb, kb, past + i * qb, k0, n_valid)
            s = jnp.where(vis[None], s.reshape(nh, qb, kb), NEG).reshape(rows, kb)
        _softmax_step(s, kblk[:, 0:LANES], m_scr, l_scr, acc_scr, 0, rows, kb)

    if causal:
        def body(j, c):
            block(j, False)
            return c
        lax.fori_loop(0, i, body, 0)
        block(i, True)
    else:
        block(0, True)

    o_lat = acc_scr[...] / l_scr[...]
    for j in range(nh // 2):
        pair = jnp.concatenate([o_lat[(2 * j) * qb:(2 * j + 1) * qb, :],
                                o_lat[(2 * j + 1) * qb:(2 * j + 2) * qb, :]], axis=1).astype(BF16)
        o_ref[0, :, LANES * j:LANES * (j + 1)] = jnp.dot(
            pair, wuv_ref[j], preferred_element_type=F32).astype(BF16)


def _mla(cq, rope_q, kcat, wqn, wqr, wqs, wuk, wuv, *, qb, kb, nh, qk_rope, past, n_valid, scale, causal):
    b, sq, q_lora = cq.shape
    sk = kcat.shape[1]
    nq = sq // qb
    rows = nh * qb
    ov = wuv.shape[0] * wuv.shape[2]
    kern = functools.partial(_mla_kernel, qb=qb, kb=kb, nh=nh, qk_rope=qk_rope, past=past,
                             n_valid=n_valid, scale=scale, causal=causal)
    return pl.pallas_call(
        kern,
        grid=(b, nq),
        in_specs=[pl.BlockSpec((1, qb, q_lora), lambda bb, i: (bb, i, 0)),
                  pl.BlockSpec((qb, rope_q.shape[1]), lambda bb, i: (i, 0)),
                  pl.BlockSpec((1, sk, kcat.shape[2]), lambda bb, i: (bb, 0, 0)),
                  _full(wqn.shape), _full(wqr.shape), _full(wqs.shape), _full(wuk.shape), _full(wuv.shape)],
        out_specs=pl.BlockSpec((1, qb, ov), lambda bb, i: (bb, i, 0)),
        out_shape=jax.ShapeDtypeStruct((b, sq, ov), BF16),
        scratch_shapes=[pltpu.VMEM((rows, 2 * LANES), BF16), pltpu.VMEM((rows, LANES), F32),
                        pltpu.VMEM((rows, LANES), F32), pltpu.VMEM((rows, LANES), F32)],
        compiler_params=_cparams(("arbitrary", "arbitrary")),
        name="mla",
    )(cq, rope_q, kcat, wqn, wqr, wqs, wuk, wuv)


def _diff_kernel(dq_ref, dk_ref, dv_ref, lam_ref, gsub_ref, o_ref, qq_scr, m_scr, l_scr, acc_scr,
                 *, qb, kb, nh, hd, past, n_valid, scale, causal, lam_init):
    i = pl.program_id(1)
    q = dq_ref[0]
    lane = lax.broadcasted_iota(I32, (qb, LANES), 1)
    zero = jnp.zeros((), BF16)
    for h in range(nh):
        qh = q[:, LANES * h:LANES * (h + 1)]
        qq_scr[(2 * h) * qb:(2 * h + 1) * qb, :] = jnp.where(lane < hd, qh, zero)
        qq_scr[(2 * h + 1) * qb:(2 * h + 2) * qb, :] = jnp.where(lane >= hd, qh, zero)
    m_scr[...] = jnp.full(m_scr.shape, NEG, F32)
    l_scr[...] = jnp.zeros(l_scr.shape, F32)
    acc_scr[...] = jnp.zeros(acc_scr.shape, F32)

    def block(j, masked):
        k0 = pl.multiple_of(j * kb, kb)
        vis, qpos, kpos = _visible(qb, kb, past + i * qb, k0, n_valid)
        dist = jnp.abs(qpos - kpos).astype(F32)
        for h in range(nh):
            slope = 2.0 ** (-8.0 * (h + 1) / nh)
            kh = dk_ref[0, pl.ds(k0, kb), LANES * h:LANES * (h + 1)]
            vh = dv_ref[0, pl.ds(k0, kb), LANES * h:LANES * (h + 1)]
            r0 = 2 * h * qb
            s = lax.dot_general(qq_scr[r0:r0 + 2 * qb, :], kh, (((1,), (1,)), ((), ())),
                                preferred_element_type=F32) * scale
            s = s.reshape(2, qb, kb) - (slope * dist)[None]
            if masked:
                s = jnp.where(vis[None], s, NEG)
            _softmax_step(s.reshape(2 * qb, kb), vh, m_scr, l_scr, acc_scr, r0, 2 * qb, kb)

    if causal:
        def body(j, c):
            block(j, False)
            return c
        lax.fori_loop(0, i, body, 0)
        block(i, True)
    else:
        block(0, True)

    lam1 = jnp.sum(lam_ref[0:1, :] * lam_ref[1:2, :], axis=1, keepdims=True)
    lam2 = jnp.sum(lam_ref[2:3, :] * lam_ref[3:4, :], axis=1, keepdims=True)
    lam = jnp.exp(lam1) - jnp.exp(lam2) + lam_init
    o_all = acc_scr[...] / l_scr[...]
    for h in range(nh):
        o = o_all[(2 * h) * qb:(2 * h + 1) * qb, :] - lam * o_all[(2 * h + 1) * qb:(2 * h + 2) * qb, :]
        o = o * lax.rsqrt(jnp.mean(o * o, axis=-1, keepdims=True) + EPS) * gsub_ref[...]
        o_ref[0, :, LANES * h:LANES * (h + 1)] = (o * (1.0 - lam_init)).astype(BF16)


def _diff(dq, dk, dv, lam4, gsub, *, qb, kb, nh, hd, past, n_valid, scale, causal, lam_init):
    b, sq, w = dq.shape
    sk = dk.shape[1]
    nq = sq // qb
    rows = 2 * nh * qb
    kern = functools.partial(_diff_kernel, qb=qb, kb=kb, nh=nh, hd=hd, past=past, n_valid=n_valid,
                             scale=scale, causal=causal, lam_init=lam_init)
    return pl.pallas_call(
        kern,
        grid=(b, nq),
        in_specs=[pl.BlockSpec((1, qb, w), lambda bb, i: (bb, i, 0)),
                  pl.BlockSpec((1, sk, w), lambda bb, i: (bb, 0, 0)),
                  pl.BlockSpec((1, sk, w), lambda bb, i: (bb, 0, 0)),
                  _full(lam4.shape), _full(gsub.shape)],
        out_specs=pl.BlockSpec((1, qb, w), lambda bb, i: (bb, i, 0)),
        out_shape=jax.ShapeDtypeStruct((b, sq, w), BF16),
        scratch_shapes=[pltpu.VMEM((rows, LANES), BF16), pltpu.VMEM((rows, LANES), F32),
                        pltpu.VMEM((rows, LANES), F32), pltpu.VMEM((rows, LANES), F32)],
        compiler_params=_cparams(("arbitrary", "arbitrary")),
        name="diffattn",
    )(dq, dk, dv, lam4, gsub)


def _red2(x, fn):
    return fn(fn(x, axis=0, keepdims=True), axis=1, keepdims=True)


def _merge_kernel(xp_ref, xs_ref, oap_ref, oas_ref, obp_ref, obs_ref, wg_ref, bg_ref, woa_ref, wob_ref,
                  wout_ref, g1_ref, b1_ref, wrt_ref, rb_ref, x1_ref, idx_ref, wts_ref, *, alpha, d, ne, ntp):
    is_p = pl.program_id(0) < ntp
    x = jnp.where(is_p, xp_ref[...], xs_ref[...])
    oa = jnp.where(is_p, oap_ref[...], oas_ref[...])
    ob = jnp.where(is_p, obp_ref[...], obs_ref[...])
    gates = jnp.dot(x.astype(BF16), wg_ref[...], preferred_element_type=F32) + bg_ref[...]
    g = jax.nn.sigmoid(gates)
    br_a = jnp.dot(oa, woa_ref[...], preferred_element_type=F32)
    br_b = jnp.dot(ob, wob_ref[...], preferred_element_type=F32)
    mixin = g[:, 0:d] * br_a + g[:, d:2 * d] * br_b
    mix = jnp.dot(mixin.astype(BF16), wout_ref[...], preferred_element_type=F32)
    z = alpha * x + mix
    mu = jnp.mean(z, axis=-1, keepdims=True)
    var = jnp.mean(jnp.square(z - mu), axis=-1, keepdims=True)
    x1 = (z - mu) * lax.rsqrt(var + EPS) * g1_ref[...] + b1_ref[...]
    x1_ref[...] = x1

    tm = x.shape[0]
    gsz = ne // N_GROUPS
    logits = lax.dot_general(wrt_ref[...], x1.astype(BF16), (((1,), (1,)), ((), ())),
                             preferred_element_type=F32)
    scores = jax.nn.sigmoid(logits)
    choice = scores + rb_ref[...]
    c3 = choice.reshape(N_GROUPS, gsz, tm)
    s3 = scores.reshape(N_GROUPS, gsz, tm)
    io = lax.broadcasted_iota(I32, (N_GROUPS, gsz, tm), 1)
    gio = lax.broadcasted_iota(I32, (N_GROUPS, gsz, tm), 0)
    eio = gio * gsz + io
    ninf = -jnp.inf
    m1 = jnp.max(c3, axis=1, keepdims=True)
    i1 = jnp.min(jnp.where(c3 == m1, io, gsz), axis=1, keepdims=True)
    m2 = jnp.max(jnp.where(io == i1, ninf, c3), axis=1, keepdims=True)
    gs = m1 + m2
    gio1 = lax.broadcasted_iota(I32, (N_GROUPS, 1, tm), 0)
    gsel = jnp.zeros((N_GROUPS, 1, tm), F32)
    cur = gs
    for _ in range(TOPK_GROUPS):
        gm = jnp.max(cur, axis=0, keepdims=True)
        gi = jnp.min(jnp.where(cur == gm, gio1, N_GROUPS), axis=0, keepdims=True)
        hit = gio1 == gi
        gsel = jnp.where(hit, 1.0, gsel)
        cur = jnp.where(hit, ninf, cur)
    cur = jnp.where(gsel > 0.5, c3, ninf)
    idx_rows, w_rows = [], []
    for _ in range(TOP_K):
        m = _red2(cur, jnp.max)
        ik = _red2(jnp.where(cur == m, eio, ne), jnp.min)
        hit = eio == ik
        w_rows.append(_red2(jnp.where(hit, s3, 0.0), jnp.sum)[0])
        idx_rows.append(ik[0])
        cur = jnp.where(hit, ninf, cur)
    wsum = w_rows[0]
    for w in w_rows[1:]:
        wsum = wsum + w
    for k in range(TOP_K):
        idx_ref[k:k + 1, :] = idx_rows[k]
        wts_ref[k:k + 1, :] = w_rows[k] / wsum * ROUTED_SCALE


def _merge(xp, xs, oap, oas, obp, obs, weights, *, tile, alpha, ne):
    n_p, d = xp.shape
    n_s = xs.shape[0]
    ntp, nts = n_p // tile, n_s // tile
    ntot = n_p + n_s
    prow = lambda c: pl.BlockSpec((tile, c), lambda i: (jnp.minimum(i, ntp - 1), 0))
    srow = lambda c: pl.BlockSpec((tile, c), lambda i: (jnp.maximum(i - ntp, 0), 0))
    wa, wb = oap.shape[1], obp.shape[1]
    return pl.pallas_call(
        functools.partial(_merge_kernel, alpha=alpha, d=d, ne=ne, ntp=ntp),
        grid=(ntp + nts,),
        in_specs=[prow(d), srow(d), prow(wa), srow(wa), prow(wb), srow(wb)] + [_full(a.shape) for a in weights],
        out_specs=[pl.BlockSpec((tile, d), lambda i: (i, 0)),
                   pl.BlockSpec((TOP_K, tile), lambda i: (0, i)),
                   pl.BlockSpec((TOP_K, tile), lambda i: (0, i))],
        out_shape=[jax.ShapeDtypeStruct((ntot, d), F32), jax.ShapeDtypeStruct((TOP_K, ntot), I32),
                   jax.ShapeDtypeStruct((TOP_K, ntot), F32)],
        compiler_params=_cparams(("arbitrary",)),
        name="merge",
    )(xp, xs, oap, oas, obp, obs, *weights)


def _plan_kernel(idx_ref, ltri_ref, su_ref, pos_ref, be_ref, bnv_ref, nused_ref, cnt_scr, base_scr,
                 *, ne, blk, nbp):
    ph = pl.program_id(0)
    i = pl.program_id(1)
    tm = idx_ref.shape[1]
    eio = lax.broadcasted_iota(I32, (ne, tm), 0)
    hits = [eio == idx_ref[k:k + 1, :] for k in range(TOP_K)]
    mh = jnp.zeros((ne, tm), F32)
    for hk in hits:
        mh = jnp.where(hk, 1.0, mh)
    rowsum = jnp.sum(mh, axis=1, keepdims=True)

    @pl.when((ph == 0) & (i == 0))
    def _():
        cnt_scr[...] = jnp.zeros(cnt_scr.shape, F32)

    @pl.when(ph == 0)
    def _():
        cnt_scr[...] = cnt_scr[...] + rowsum

    @pl.when((ph == 1) & (i == 0))
    def _():
        cnt = cnt_scr[...]
        sh = blk.bit_length() - 1
        nb = lax.shift_right_logical(cnt.astype(I32) + (blk - 1), sh)
        nbf = nb.astype(F32)
        start = jnp.dot(ltri_ref[...], jnp.broadcast_to(nbf, (ne, LANES)).astype(BF16),
                        preferred_element_type=F32)[:, 0:1]
        end = start + nbf
        base_scr[...] = start * float(blk)
        bio = lax.broadcasted_iota(I32, (ne, nbp), 1).astype(F32)
        be = jnp.sum(jnp.where(end <= bio, 1.0, 0.0), axis=0, keepdims=True)
        be_ref[...] = jnp.minimum(be, float(ne - 1)).astype(I32)
        inside = (start <= bio) & (bio < end)
        left = jnp.clip(cnt - (bio - start) * float(blk), 0.0, float(blk))
        bnv_ref[...] = jnp.sum(jnp.where(inside, left, 0.0), axis=0, keepdims=True).astype(I32)
        nused_ref[...] = jnp.broadcast_to(jnp.sum(nbf, axis=0, keepdims=True), (1, LANES)).astype(I32)

    @pl.when(ph == 1)
    def _():
        before = jnp.dot(mh.astype(BF16), su_ref[...], preferred_element_type=F32)
        rank = base_scr[...] + before
        for k in range(TOP_K):
            pos_ref[k:k + 1, :] = jnp.sum(jnp.where(hits[k], rank, 0.0), axis=0, keepdims=True).astype(I32)
        base_scr[...] = base_scr[...] + rowsum


def _plan(idx, *, ne, blk, nbp, tile):
    ntot = idx.shape[1]
    nt = ntot // tile
    ltri = jnp.tril(jnp.ones((ne, ne), F32), -1).astype(BF16)
    su = jnp.triu(jnp.ones((tile, tile), F32), 1).astype(BF16)
    kern = functools.partial(_plan_kernel, ne=ne, blk=blk, nbp=nbp)
    meta = lambda: pl.BlockSpec((1, nbp), lambda p, i: (0, 0))
    return pl.pallas_call(
        kern,
        grid=(2, nt),
        in_specs=[pl.BlockSpec((TOP_K, tile), lambda p, i: (0, i)), _full((ne, ne)), _full((tile, tile))],
        out_specs=[pl.BlockSpec((TOP_K, tile), lambda p, i: (0, i * p)), meta(), meta(),
                   pl.BlockSpec((1, LANES), lambda p, i: (0, 0))],
        out_shape=[jax.ShapeDtypeStruct((TOP_K, ntot), I32), jax.ShapeDtypeStruct((1, nbp), I32),
                   jax.ShapeDtypeStruct((1, nbp), I32), jax.ShapeDtypeStruct((1, LANES), I32)],
        scratch_shapes=[pltpu.VMEM((ne, 1), F32), pltpu.VMEM((ne, 1), F32)],
        compiler_params=_cparams(("arbitrary", "arbitrary")),
        name="plan",
    )(idx, ltri, su)


def _scatter_kernel(pos_ref, x_ref, xs_ref, sem):
    tm = x_ref.shape[0]

    def tok(t, c):
        for k in range(TOP_K):
            p = pos_ref[k, t]
            pltpu.make_async_copy(x_ref.at[pl.ds(t, 1), :], xs_ref.at[pl.ds(p, 1), :], sem).start()
        return c

    lax.fori_loop(0, tm, tok, 0)
    for _ in range(TOP_K):
        pltpu.make_async_copy(x_ref, xs_ref.at[pl.ds(0, tm), :], sem).wait()


def _scatter(pos, x1, *, rows, tile):
    ntot, d = x1.shape
    return pl.pallas_call(
        _scatter_kernel,
        grid=(ntot // tile,),
        in_specs=[pl.BlockSpec((TOP_K, tile), lambda i: (0, i), memory_space=pltpu.SMEM),
                  pl.BlockSpec((tile, d), lambda i: (i, 0))],
        out_specs=pl.BlockSpec(memory_space=pl.ANY),
        out_shape=jax.ShapeDtypeStruct((rows, d), F32),
        scratch_shapes=[pltpu.SemaphoreType.DMA],
        compiler_params=_cparams(("arbitrary",)),
        name="scatter",
    )(pos, x1)


def _ffn_kernel(be_ref, bnv_ref, nused_ref, xs_ref, wg_ref, wu_ref, wd_ref, ys_ref):
    b = pl.program_id(0)

    @pl.when(b < nused_ref[0])
    def _():
        rows = xs_ref.shape[0]
        rid = lax.broadcasted_iota(I32, (rows, 1), 0)
        x = jnp.where(rid < bnv_ref[b], xs_ref[...], 0.0).astype(BF16)
        g = jnp.dot(x, wg_ref[0].astype(BF16), preferred_element_type=F32)
        u = jnp.dot(x, wu_ref[0].astype(BF16), preferred_element_type=F32)
        h = (g * jax.nn.sigmoid(g)) * u
        ys_ref[...] = jnp.dot(h.astype(BF16), wd_ref[0].astype(BF16), preferred_element_type=F32)


def _ffn(be, bnv, nused, xs, wg, wu, wd, *, blk):
    rows, d = xs.shape
    de = wg.shape[2]
    nb = rows // blk
    rows_map = lambda b, be, bnv, nu: (jnp.minimum(b, nu[0] - 1), 0)
    w_map = lambda b, be, bnv, nu: (be[jnp.minimum(b, nu[0] - 1)], 0, 0)
    grid_spec = pltpu.PrefetchScalarGridSpec(
        num_scalar_prefetch=3,
        grid=(nb,),
        in_specs=[pl.BlockSpec((blk, d), rows_map),
                  pl.BlockSpec((1, d, de), w_map),
                  pl.BlockSpec((1, d, de), w_map),
                  pl.BlockSpec((1, de, d), w_map)],
        out_specs=pl.BlockSpec((blk, d), rows_map),
    )
    return pl.pallas_call(
        _ffn_kernel,
        grid_spec=grid_spec,
        out_shape=jax.ShapeDtypeStruct((rows, d), F32),
        compiler_params=_cparams(("arbitrary",)),
        name="ffn",
    )(be, bnv, nused, xs, wg, wu, wd)


def _final_kernel(pos_ref, x1_ref, wts_ref, ys_ref, wsg_ref, wsu_ref, wsd_ref, g2_ref, b2_ref, y_ref,
                  gbuf, sem, *, alpha):
    tm = x1_ref.shape[0]

    def tok(t, c):
        for k in range(TOP_K):
            p = pos_ref[k, t]
            pltpu.make_async_copy(ys_ref.at[pl.ds(p, 1), :], gbuf.at[k, pl.ds(t, 1), :], sem).start()
        return c

    lax.fori_loop(0, tm, tok, 0)
    x1 = x1_ref[...]
    xb = x1.astype(BF16)
    g = jnp.dot(xb, wsg_ref[...], preferred_element_type=F32)
    u = jnp.dot(xb, wsu_ref[...], preferred_element_type=F32)
    shared = jnp.dot(((g * jax.nn.sigmoid(g)) * u).astype(BF16), wsd_ref[...], preferred_element_type=F32)
    eye = lax.broadcasted_iota(I32, (tm, tm), 0) == lax.broadcasted_iota(I32, (tm, tm), 1)
    for k in range(TOP_K):
        pltpu.make_async_copy(ys_ref.at[pl.ds(0, tm), :], gbuf.at[k], sem).wait()
    routed = jnp.zeros(x1.shape, F32)
    for k in range(TOP_K):
        wcol = jnp.sum(jnp.where(eye, wts_ref[k:k + 1, :], 0.0), axis=1, keepdims=True)
        routed = routed + gbuf[k] * wcol
    z = alpha * x1 + (routed + shared)
    mu = jnp.mean(z, axis=-1, keepdims=True)
    var = jnp.mean(jnp.square(z - mu), axis=-1, keepdims=True)
    y_ref[...] = (z - mu) * lax.rsqrt(var + EPS) * g2_ref[...] + b2_ref[...]


def _final(pos, x1, wts, ys, wsg, wsu, wsd, g2, b2, *, tile, off, n, alpha):
    d = x1.shape[1]
    kern = functools.partial(_final_kernel, alpha=alpha)
    return pl.pallas_call(
        kern,
        grid=(n // tile,),
        in_specs=[pl.BlockSpec((TOP_K, tile), lambda i: (0, i + off), memory_space=pltpu.SMEM),
                  pl.BlockSpec((tile, d), lambda i: (i + off, 0)),
                  pl.BlockSpec((TOP_K, tile), lambda i: (0, i + off)),
                  pl.BlockSpec(memory_space=pl.ANY),
                  _full(wsg.shape), _full(wsu.shape), _full(wsd.shape), _full(g2.shape), _full(b2.shape)],
        out_specs=pl.BlockSpec((tile, d), lambda i: (i, 0)),
        out_shape=jax.ShapeDtypeStruct((n, d), F32),
        scratch_shapes=[pltpu.VMEM((TOP_K, tile, d), F32), pltpu.SemaphoreType.DMA],
        compiler_params=_cparams(("arbitrary",)),
        name="final",
    )(pos, x1, wts, ys, wsg, wsu, wsd, g2, b2)


def _rope_tables(pos, qk_rope, nh):
    half = qk_rope // 2
    inv = ROPE_BASE ** (-jnp.arange(half, dtype=F32) / half)
    ang = pos.astype(F32)[:, None] * inv
    cos, sin = jnp.cos(ang), jnp.sin(ang)
    c = jnp.concatenate([cos, cos], axis=1)
    s = jnp.concatenate([-sin, sin], axis=1)
    rep = LANES // qk_rope
    rope_k = jnp.concatenate([jnp.tile(c, (1, rep)), jnp.tile(s, (1, rep))], axis=1)
    rope_q = jnp.concatenate([jnp.tile(c, (1, nh)), jnp.tile(s, (1, nh))], axis=1)
    return rope_k, rope_q


def _swap_halves(w, axis):
    a, b = jnp.split(w, 2, axis=axis)
    return jnp.concatenate([b, a], axis=axis)


def _block_diag_pairs(w):
    nh, r, c = w.shape
    z = jnp.zeros((nh // 2, r, c), w.dtype)
    top = jnp.concatenate([w[0::2], z], axis=2)
    bot = jnp.concatenate([z, w[1::2]], axis=2)
    return jnp.concatenate([top, bot], axis=1)


def kernel(x_prompt, x_sample, cache_mla_latent, cache_mla_krope, cache_diff_k, cache_diff_v, w_in, b_gate, g_q_norm, w_uq, w_uk, g_kv_norm, w_uv, w_o_mla, lambda_q1, lambda_k1, lambda_q2, lambda_k2, g_subln, w_o_diff, w_out, g_ln1, b_ln1, w_router, router_bias, w_exp_gate, w_exp_up, w_exp_down, w_sh_gate, w_sh_up, w_sh_down, g_ln2, b_ln2):
    depth = w_in.shape[0]
    assert depth == 1
    b, s, d = x_prompt.shape
    bs, ss, _ = x_sample.shape
    past = cache_mla_latent.shape[2]
    q_lora = g_q_norm.shape[1]
    kv_lora = g_kv_norm.shape[1]
    qk_rope = cache_mla_krope.shape[3]
    nh = w_uq.shape[2]
    qk_nope = w_uq.shape[3] - qk_rope
    mla_v = w_uv.shape[3]
    dnh, _, hd = cache_diff_k.shape[3:]
    diff_w = dnh * 2 * hd
    ne = w_router.shape[2]
    alpha = (2 * depth) ** 0.25
    lam_init = 0.8 - 0.6 * math.exp(-0.3 * 0)
    assert 2 * hd == LANES and kv_lora == LANES and 2 * qk_nope == LANES and LANES % qk_rope == 0

    wi = w_in[0]
    c0 = q_lora + kv_lora
    w_cq, w_ckv = wi[:, :q_lora], wi[:, q_lora:c0]
    w_kr = wi[:, c0:c0 + qk_rope]
    c1 = c0 + qk_rope
    w_d = wi[:, c1:c1 + 3 * diff_w]
    w_gates = wi[:, c1 + 3 * diff_w:]
    rep = LANES // qk_rope
    w1 = jnp.concatenate([w_cq, w_ckv, w_d, jnp.tile(w_kr, (1, rep)),
                          jnp.tile(_swap_halves(w_kr, 1), (1, rep))], axis=1).astype(BF16)
    uq = w_uq[0]
    wqn = uq[:, :, :qk_nope].reshape(q_lora, nh * qk_nope).astype(BF16)
    wqr = uq[:, :, qk_nope:].reshape(q_lora, nh * qk_rope).astype(BF16)
    wqs = _swap_halves(uq[:, :, qk_nope:], 2).reshape(q_lora, nh * qk_rope).astype(BF16)
    wuk = _block_diag_pairs(jnp.transpose(w_uk[0], (1, 2, 0))).astype(BF16)
    wuv = _block_diag_pairs(jnp.transpose(w_uv[0], (1, 0, 2))).astype(BF16)
    lam4 = jnp.concatenate([lambda_q1, lambda_k1, lambda_q2, lambda_k2], axis=0)
    merge_w = (w_gates.astype(BF16), b_gate, w_o_mla[0].astype(BF16), w_o_diff[0].astype(BF16),
               w_out[0].astype(BF16), g_ln1, b_ln1, jnp.transpose(w_router[0]).astype(BF16),
               jnp.transpose(router_bias))
    dims = (q_lora, kv_lora, diff_w, qk_rope)
    mla_scale = (qk_nope + qk_rope) ** -0.5
    diff_scale = hd ** -0.5

    n_p = b * s
    rope_k, rope_q = _rope_tables(jnp.arange(s, dtype=I32), qk_rope, nh)
    cq, lat, kr, kcat, dq, dk, dkb, dv, dvb = _proj(
        x_prompt.reshape(n_p, d), w1, g_q_norm, g_kv_norm, rope_k, tile=PROJ_TILE, dims=dims)
    r3 = lambda a: a.reshape(b, s, a.shape[1])
    oa = _mla(r3(cq), rope_q, r3(kcat), wqn, wqr, wqs, wuk, wuv, qb=ATT_BLOCK, kb=ATT_BLOCK, nh=nh,
              qk_rope=qk_rope, past=0, n_valid=s, scale=mla_scale, causal=True)
    ob = _diff(r3(dq), r3(dkb), r3(dvb), lam4, g_subln, qb=ATT_BLOCK, kb=ATT_BLOCK, nh=dnh, hd=hd,
               past=0, n_valid=s, scale=diff_scale, causal=True, lam_init=lam_init)

    n_s = bs * ss
    sk = past + ss
    skp = -(-sk // LANES) * LANES
    rope_ks, rope_qs = _rope_tables(past + jnp.arange(ss, dtype=I32), qk_rope, nh)
    cq_s, lat_s, kr_s, kcat_s, dq_s, dk_s, dkb_s, dv_s, dvb_s = _proj(
        x_sample.reshape(n_s, d), w1, g_q_norm, g_kv_norm, jnp.tile(rope_ks, (bs, 1)), tile=n_s, dims=dims)
    r3s = lambda a: a.reshape(bs, ss, a.shape[1])
    padk = lambda a: jnp.pad(a, ((0, 0), (0, skp - sk), (0, 0)))
    kcat_all = padk(jnp.concatenate(
        [jnp.concatenate([cache_mla_latent[0], jnp.tile(cache_mla_krope[0], (1, 1, rep))], axis=2).astype(BF16),
         r3s(kcat_s)], axis=1))
    dk_all = padk(jnp.concatenate([cache_diff_k[0].reshape(bs, past, diff_w).astype(BF16), r3s(dkb_s)], axis=1))
    dv_all = padk(jnp.concatenate([cache_diff_v[0].reshape(bs, past, diff_w).astype(BF16), r3s(dvb_s)], axis=1))
    oa_s = _mla(r3s(cq_s), rope_qs, kcat_all, wqn, wqr, wqs, wuk, wuv, qb=ss, kb=skp, nh=nh,
                qk_rope=qk_rope, past=past, n_valid=sk, scale=mla_scale, causal=False)
    ob_s = _diff(r3s(dq_s), dk_all, dv_all, lam4, g_subln, qb=ss, kb=skp, nh=dnh, hd=hd,
                 past=past, n_valid=sk, scale=diff_scale, causal=False, lam_init=lam_init)

    ntot = n_p + n_s
    x1, idx, wts = _merge(x_prompt.reshape(n_p, d), x_sample.reshape(n_s, d), oa.reshape(n_p, -1),
                          oa_s.reshape(n_s, -1), ob.reshape(n_p, -1), ob_s.reshape(n_s, -1), merge_w,
                          tile=TOK_TILE, alpha=alpha, ne=ne)
    nblocks = -(-(ntot * TOP_K) // EXP_BLOCK) + ne
    nbp = -(-nblocks // LANES) * LANES
    pos, be, bnv, nused = _plan(idx, ne=ne, blk=EXP_BLOCK, nbp=nbp, tile=TOK_TILE)
    xs = _scatter(pos, x1, rows=nblocks * EXP_BLOCK, tile=TOK_TILE)
    ys = _ffn(be[0, :nblocks], bnv[0, :nblocks], nused[0, :1], xs, w_exp_gate[0], w_exp_up[0], w_exp_down[0],
              blk=EXP_BLOCK)
    fin_w = (w_sh_gate[0].astype(BF16), w_sh_up[0].astype(BF16), w_sh_down[0].astype(BF16), g_ln2, b_ln2)
    y_p = _final(pos, x1, wts, ys, *fin_w, tile=TOK_TILE, off=0, n=n_p, alpha=alpha)
    y_s = _final(pos, x1, wts, ys, *fin_w, tile=TOK_TILE, off=n_p // TOK_TILE, n=n_s, alpha=alpha)

    st = lambda a, bb, sq, tail: a.reshape((1, bb, sq) + tail)
    return (y_p.reshape(b, s, d), y_s.reshape(bs, ss, d),
            st(lat, b, s, (kv_lora,)), st(kr, b, s, (qk_rope,)),
            st(dk, b, s, (dnh, 2, hd)), st(dv, b, s, (dnh, 2 * hd)),
            st(lat_s, bs, ss, (kv_lora,)), st(kr_s, bs, ss, (qk_rope,)),
            st(dk_s, bs, ss, (dnh, 2, hd)), st(dv_s, bs, ss, (dnh, 2 * hd)))
```

```python
import functools
import math

import jax
import jax.numpy as jnp
from jax import lax
from jax.experimental import pallas as pl
from jax.experimental.pallas import tpu as pltpu

F32 = jnp.float32
BF16 = jnp.bfloat16
I32 = jnp.int32

CHUNK = 64
ROPE_BASE = 10000.0
EPS = 1e-6
N_GROUPS = 8
TOPK_GROUPS = 4
TOP_K = 8
ROUTED_SCALE = 2.5
NEG = -1e30

LANES = 128
TOK_TILE = 256
PROJ_TILE = 512
ATT_BLOCK = 256
EXP_BLOCK = 512
VMEM_LIMIT = 56 * 1024 * 1024


def _cparams(sem):
    return pltpu.CompilerParams(dimension_semantics=sem, vmem_limit_bytes=VMEM_LIMIT)


def _full(shape):
    nd = len(shape)
    return pl.BlockSpec(shape, lambda *_: (0,) * nd)


def _proj_kernel(x_ref, w_ref, gq_ref, gkv_ref, rope_ref,
                 cq_ref, lat_ref, kr_ref, kcat_ref, dq_ref, dk_ref, dkb_ref, dv_ref, dvb_ref,
                 *, q_lora, kv_lora, diff_w, qk_rope):
    x = x_ref[...].astype(BF16)
    h = jnp.dot(x, w_ref[...], preferred_element_type=F32)
    o = 0
    cq = h[:, o:o + q_lora]
    o += q_lora
    cq = cq * lax.rsqrt(jnp.mean(cq * cq, axis=-1, keepdims=True) + EPS) * gq_ref[...]
    cq_ref[...] = cq.astype(BF16)
    ckv = h[:, o:o + kv_lora]
    o += kv_lora
    lat = ckv * lax.rsqrt(jnp.mean(ckv * ckv, axis=-1, keepdims=True) + EPS) * gkv_ref[...]
    lat_ref[...] = lat
    dq_ref[...] = h[:, o:o + diff_w].astype(BF16)
    o += diff_w
    dk = h[:, o:o + diff_w]
    o += diff_w
    dk_ref[...] = dk
    dkb_ref[...] = dk.astype(BF16)
    dv = h[:, o:o + diff_w]
    o += diff_w
    dv_ref[...] = dv
    dvb_ref[...] = dv.astype(BF16)
    krr = h[:, o:o + LANES] * rope_ref[:, 0:LANES] + h[:, o + LANES:o + 2 * LANES] * rope_ref[:, LANES:2 * LANES]
    kr_ref[...] = krr[:, 0:qk_rope]
    kcat_ref[...] = jnp.concatenate([lat, krr], axis=1).astype(BF16)


def _proj(x, w1, gq, gkv, rope_k, *, tile, dims):
    n, d = x.shape
    q_lora, kv_lora, diff_w, qk_rope = dims
    wcols = w1.shape[1]
    nrope = rope_k.shape[0] // tile
    row = lambda c: pl.BlockSpec((tile, c), lambda i: (i, 0))
    outs = [
        jax.ShapeDtypeStruct((n, q_lora), BF16),
        jax.ShapeDtypeStruct((n, kv_lora), F32),
        jax.ShapeDtypeStruct((n, qk_rope), F32),
        jax.ShapeDtypeStruct((n, 2 * LANES), BF16),
        jax.ShapeDtypeStruct((n, diff_w), BF16),
        jax.ShapeDtypeStruct((n, diff_w), F32),
        jax.ShapeDtypeStruct((n, diff_w), BF16),
        jax.ShapeDtypeStruct((n, diff_w), F32),
        jax.ShapeDtypeStruct((n, diff_w), BF16),
    ]
    return pl.pallas_call(
        functools.partial(_proj_kernel, q_lora=q_lora, kv_lora=kv_lora, diff_w=diff_w, qk_rope=qk_rope),
        grid=(n // tile,),
        in_specs=[row(d), _full((d, wcols)), _full((1, q_lora)), _full((1, kv_lora)),
                  pl.BlockSpec((tile, 2 * LANES), lambda i: (i % nrope, 0))],
        out_specs=[row(q_lora), row(kv_lora), row(qk_rope), row(2 * LANES), row(diff_w), row(diff_w),
                   row(diff_w), row(diff_w), row(diff_w)],
        out_shape=outs,
        compiler_params=_cparams(("arbitrary",)),
        name="proj",
    )(x, w1, gq, gkv, rope_k)


def _softmax_step(s, v, m_scr, l_scr, acc_scr, r0, rows, kb):
    m_prev = m_scr[r0:r0 + rows, :]
    l_prev = l_scr[r0:r0 + rows, :]
    m_cur = jnp.max(s, axis=1, keepdims=True)
    m_next = jnp.maximum(m_prev, m_cur)
    p = jnp.exp(s - jnp.concatenate([m_next] * (kb // LANES), axis=1))
    alpha = jnp.exp(m_prev - m_next)
    l_scr[r0:r0 + rows, :] = alpha * l_prev + jnp.sum(p, axis=1, keepdims=True)
    m_scr[r0:r0 + rows, :] = m_next
    pv = jnp.dot(p.astype(BF16), v, preferred_element_type=F32)
    acc_scr[r0:r0 + rows, :] = acc_scr[r0:r0 + rows, :] * alpha + pv


def _visible(qb, kb, q0, k0, n_valid):
    qpos = q0 + lax.broadcasted_iota(I32, (qb, kb), 0)
    kpos = k0 + lax.broadcasted_iota(I32, (qb, kb), 1)
    sh = CHUNK.bit_length() - 1
    vis = (lax.shift_right_logical(kpos, sh) <= lax.shift_right_logical(qpos, sh)) & (kpos < n_valid)
    return vis, qpos, kpos


def _mla_kernel(cq_ref, ropeq_ref, kcat_ref, wqn_ref, wqr_ref, wqs_ref, wuk_ref, wuv_ref, o_ref,
                qcat_scr, m_scr, l_scr, acc_scr, *, qb, kb, nh, qk_rope, past, n_valid, scale, causal):
    i = pl.program_id(1)
    rows = nh * qb
    cq = cq_ref[0]
    qn = jnp.dot(cq, wqn_ref[...], preferred_element_type=F32).astype(BF16)
    qr = jnp.dot(cq, wqr_ref[...], preferred_element_type=F32)
    qs = jnp.dot(cq, wqs_ref[...], preferred_element_type=F32)
    rw = nh * qk_rope
    rot = qr * ropeq_ref[:, 0:rw] + qs * ropeq_ref[:, rw:2 * rw]
    lane = lax.broadcasted_iota(I32, (qb, LANES), 1)
    per = LANES // qk_rope
    for j in range(nh // 2):
        ql = jnp.dot(qn[:, LANES * j:LANES * (j + 1)], wuk_ref[j], preferred_element_type=F32)
        for u in range(2):
            hh = 2 * j + u
            rblk = rot[:, LANES * (hh // per):LANES * (hh // per + 1)]
            lo = qk_rope * (hh % per)
            part2 = jnp.where((lane >= lo) & (lane < lo + qk_rope), rblk, 0.0)
            qcat_scr[hh * qb:(hh + 1) * qb, :] = jnp.concatenate(
                [ql[:, LANES * u:LANES * (u + 1)], part2], axis=1).astype(BF16)
    m_scr[...] = jnp.full(m_scr.shape, NEG, F32)
    l_scr[...] = jnp.zeros(l_scr.shape, F32)
    acc_scr[...] = jnp.zeros(acc_scr.shape, F32)

    def block(j, masked):
        k0 = pl.multiple_of(j * kb, kb)
        kblk = kcat_ref[0, pl.ds(k0, kb), :]
        s = lax.dot_general(qcat_scr[...], kblk, (((1,), (1,)), ((), ())),
                            preferred_element_type=F32) * scale
        if masked:
            vis, _, _ = _visible(qb, kb, past + i * qb, k0, n_valid)
            s = jnp.where(vis[None], s.reshape(nh, qb, kb), NEG).reshape(rows, kb)
        _softmax_step(s, kblk[:, 0:LANES], m_scr, l_scr, acc_scr, 0, rows, kb)

    if causal:
        def body(j, c):
            block(j, False)
            return c
        lax.fori_loop(0, i, body, 0)
        block(i, True)
    else:
        block(0, True)

    o_lat = acc_scr[...] / l_scr[...]
    for j in range(nh // 2):
        pair = jnp.concatenate([o_lat[(2 * j) * qb:(2 * j + 1) * qb, :],
                                o_lat[(2 * j + 1) * qb:(2 * j + 2) * qb, :]], axis=1).astype(BF16)
        o_ref[0, :, LANES * j:LANES * (j + 1)] = jnp.dot(
            pair, wuv_ref[j], preferred_element_type=F32).astype(BF16)


def _mla(cq, rope_q, kcat, wqn, wqr, wqs, wuk, wuv, *, qb, kb, nh, qk_rope, past, n_valid, scale, causal):
    b, sq, q_lora = cq.shape
    sk = kcat.shape[1]
    nq = sq // qb
    rows = nh * qb
    ov = wuv.shape[0] * wuv.shape[2]
    kern = functools.partial(_mla_kernel, qb=qb, kb=kb, nh=nh, qk_rope=qk_rope, past=past,
                             n_valid=n_valid, scale=scale, causal=causal)
    return pl.pallas_call(
        kern,
        grid=(b, nq),
        in_specs=[pl.BlockSpec((1, qb, q_lora), lambda bb, i: (bb, i, 0)),
                  pl.BlockSpec((qb, rope_q.shape[1]), lambda bb, i: (i, 0)),
                  pl.BlockSpec((1, sk, kcat.shape[2]), lambda bb, i: (bb, 0, 0)),
                  _full(wqn.shape), _full(wqr.shape), _full(wqs.shape), _full(wuk.shape), _full(wuv.shape)],
        out_specs=pl.BlockSpec((1, qb, ov), lambda bb, i: (bb, i, 0)),
        out_shape=jax.ShapeDtypeStruct((b, sq, ov), BF16),
        scratch_shapes=[pltpu.VMEM((rows, 2 * LANES), BF16), pltpu.VMEM((rows, LANES), F32),
                        pltpu.VMEM((rows, LANES), F32), pltpu.VMEM((rows, LANES), F32)],
        compiler_params=_cparams(("arbitrary", "arbitrary")),
        name="mla",
    )(cq, rope_q, kcat, wqn, wqr, wqs, wuk, wuv)


def _diff_kernel(dq_ref, dk_ref, dv_ref, lam_ref, gsub_ref, o_ref, qq_scr, m_scr, l_scr, acc_scr,
                 *, qb, kb, nh, hd, past, n_valid, scale, causal, lam_init):
    i = pl.program_id(1)
    q = dq_ref[0]
    lane = lax.broadcasted_iota(I32, (qb, LANES), 1)
    zero = jnp.zeros((), BF16)
    for h in range(nh):
        qh = q[:, LANES * h:LANES * (h + 1)]
        qq_scr[(2 * h) * qb:(2 * h + 1) * qb, :] = jnp.where(lane < hd, qh, zero)
        qq_scr[(2 * h + 1) * qb:(2 * h + 2) * qb, :] = jnp.where(lane >= hd, qh, zero)
    m_scr[...] = jnp.full(m_scr.shape, NEG, F32)
    l_scr[...] = jnp.zeros(l_scr.shape, F32)
    acc_scr[...] = jnp.zeros(acc_scr.shape, F32)

    def block(j, masked):
        k0 = pl.multiple_of(j * kb, kb)
        vis, qpos, kpos = _visible(qb, kb, past + i * qb, k0, n_valid)
        dist = jnp.abs(qpos - kpos).astype(F32)
        for h in range(nh):
            slope = 2.0 ** (-8.0 * (h + 1) / nh)
            kh = dk_ref[0, pl.ds(k0, kb), LANES * h:LANES * (h + 1)]
            vh = dv_ref[0, pl.ds(k0, kb), LANES * h:LANES * (h + 1)]
            r0 = 2 * h * qb
            s = lax.dot_general(qq_scr[r0:r0 + 2 * qb, :], kh, (((1,), (1,)), ((), ())),
                                preferred_element_type=F32) * scale
            s = s.reshape(2, qb, kb) - (slope * dist)[None]
            if masked:
                s = jnp.where(vis[None], s, NEG)
            _softmax_step(s.reshape(2 * qb, kb), vh, m_scr, l_scr, acc_scr, r0, 2 * qb, kb)

    if causal:
        def body(j, c):
            block(j, False)
            return c
        lax.fori_loop(0, i, body, 0)
        block(i, True)
    else:
        block(0, True)

    lam1 = jnp.sum(lam_ref[0:1, :] * lam_ref[1:2, :], axis=1, keepdims=True)
    lam2 = jnp.sum(lam_ref[2:3, :] * lam_ref[3:4, :], axis=1, keepdims=True)
    lam = jnp.exp(lam1) - jnp.exp(lam2) + lam_init
    o_all = acc_scr[...] / l_scr[...]
    for h in range(nh):
        o = o_all[(2 * h) * qb:(2 * h + 1) * qb, :] - lam * o_all[(2 * h + 1) * qb:(2 * h + 2) * qb, :]
        o = o * lax.rsqrt(jnp.mean(o * o, axis=-1, keepdims=True) + EPS) * gsub_ref[...]
        o_ref[0, :, LANES * h:LANES * (h + 1)] = (o * (1.0 - lam_init)).astype(BF16)


def _diff(dq, dk, dv, lam4, gsub, *, qb, kb, nh, hd, past, n_valid, scale, causal, lam_init):
    b, sq, w = dq.shape
    sk = dk.shape[1]
    nq = sq // qb
    rows = 2 * nh * qb
    kern = functools.partial(_diff_kernel, qb=qb, kb=kb, nh=nh, hd=hd, past=past, n_valid=n_valid,
                             scale=scale, causal=causal, lam_init=lam_init)
    return pl.pallas_call(
        kern,
        grid=(b, nq),
        in_specs=[pl.BlockSpec((1, qb, w), lambda bb, i: (bb, i, 0)),
                  pl.BlockSpec((1, sk, w), lambda bb, i: (bb, 0, 0)),
                  pl.BlockSpec((1, sk, w), lambda bb, i: (bb, 0, 0)),
                  _full(lam4.shape), _full(gsub.shape)],
        out_specs=pl.BlockSpec((1, qb, w), lambda bb, i: (bb, i, 0)),
        out_shape=jax.ShapeDtypeStruct((b, sq, w), BF16),
        scratch_shapes=[pltpu.VMEM((rows, LANES), BF16), pltpu.VMEM((rows, LANES), F32),
                        pltpu.VMEM((rows, LANES), F32), pltpu.VMEM((rows, LANES), F32)],
        compiler_params=_cparams(("arbitrary", "arbitrary")),
        name="diffattn",
    )(dq, dk, dv, lam4, gsub)


def _red2(x, fn):
    return fn(fn(x, axis=0, keepdims=True), axis=1, keepdims=True)


PACK_ROWS = 4


def _store_packed(ref, x):
    rows, d = x.shape
    bits = pltpu.bitcast(x.astype(BF16).astype(F32), jnp.uint32)
    packed = bits[:, 0:d // 2] | (bits[:, d // 2:d] >> 16)
    for j in range(PACK_ROWS):
        ref[pl.ds(j, rows, stride=PACK_ROWS), :] = packed[:, LANES * j:LANES * (j + 1)]


def _load_packed(ref, rows):
    words = [ref[pl.ds(j, rows, stride=PACK_ROWS), :] for j in range(PACK_ROWS)]
    hi = [pltpu.bitcast(w & jnp.uint32(0xFFFF0000), F32) for w in words]
    lo = [pltpu.bitcast(w << 16, F32) for w in words]
    return jnp.concatenate(hi + lo, axis=1)


def _merge_kernel(xp_ref, xs_ref, oap_ref, oas_ref, obp_ref, obs_ref, wg_ref, bg_ref, woa_ref, wob_ref,
                  wout_ref, g1_ref, b1_ref, wrt_ref, rb_ref, x1_ref, x1p_ref, idx_ref, wts_ref, *, alpha, d, ne, ntp):
    is_p = pl.program_id(0) < ntp
    x = jnp.where(is_p, xp_ref[...], xs_ref[...])
    oa = jnp.where(is_p, oap_ref[...], oas_ref[...])
    ob = jnp.where(is_p, obp_ref[...], obs_ref[...])
    gates = jnp.dot(x.astype(BF16), wg_ref[...], preferred_element_type=F32) + bg_ref[...]
    g = jax.nn.sigmoid(gates)
    br_a = jnp.dot(oa, woa_ref[...], preferred_element_type=F32)
    br_b = jnp.dot(ob, wob_ref[...], preferred_element_type=F32)
    mixin = g[:, 0:d] * br_a + g[:, d:2 * d] * br_b
    mix = jnp.dot(mixin.astype(BF16), wout_ref[...], preferred_element_type=F32)
    z = alpha * x + mix
    mu = jnp.mean(z, axis=-1, keepdims=True)
    var = jnp.mean(jnp.square(z - mu), axis=-1, keepdims=True)
    x1 = (z - mu) * lax.rsqrt(var + EPS) * g1_ref[...] + b1_ref[...]
    x1_ref[...] = x1
    _store_packed(x1p_ref, x1)

    tm = x.shape[0]
    gsz = ne // N_GROUPS
    logits = lax.dot_general(wrt_ref[...], x1.astype(BF16), (((1,), (1,)), ((), ())),
                             preferred_element_type=F32)
    scores = jax.nn.sigmoid(logits)
    choice = scores + rb_ref[...]
    c3 = choice.reshape(N_GROUPS, gsz, tm)
    s3 = scores.reshape(N_GROUPS, gsz, tm)
    io = lax.broadcasted_iota(I32, (N_GROUPS, gsz, tm), 1)
    gio = lax.broadcasted_iota(I32, (N_GROUPS, gsz, tm), 0)
    eio = gio * gsz + io
    ninf = -jnp.inf
    m1 = jnp.max(c3, axis=1, keepdims=True)
    i1 = jnp.min(jnp.where(c3 == m1, io, gsz), axis=1, keepdims=True)
    m2 = jnp.max(jnp.where(io == i1, ninf, c3), axis=1, keepdims=True)
    gs = m1 + m2
    gio1 = lax.broadcasted_iota(I32, (N_GROUPS, 1, tm), 0)
    gsel = jnp.zeros((N_GROUPS, 1, tm), F32)
    cur = gs
    for _ in range(TOPK_GROUPS):
        gm = jnp.max(cur, axis=0, keepdims=True)
        gi = jnp.min(jnp.where(cur == gm, gio1, N_GROUPS), axis=0, keepdims=True)
        hit = gio1 == gi
        gsel = jnp.where(hit, 1.0, gsel)
        cur = jnp.where(hit, ninf, cur)
    cur = jnp.where(gsel > 0.5, c3, ninf)
    idx_rows, w_rows = [], []
    for _ in range(TOP_K):
        m = _red2(cur, jnp.max)
        ik = _red2(jnp.where(cur == m, eio, ne), jnp.min)
        hit = eio == ik
        w_rows.append(_red2(jnp.where(hit, s3, 0.0), jnp.sum)[0])
        idx_rows.append(ik[0])
        cur = jnp.where(hit, ninf, cur)
    wsum = w_rows[0]
    for w in w_rows[1:]:
        wsum = wsum + w
    for k in range(TOP_K):
        idx_ref[k:k + 1, :] = idx_rows[k]
        wts_ref[k:k + 1, :] = w_rows[k] / wsum * ROUTED_SCALE


def _merge(xp, xs, oap, oas, obp, obs, weights, *, tile, alpha, ne):
    n_p, d = xp.shape
    n_s = xs.shape[0]
    ntp, nts = n_p // tile, n_s // tile
    ntot = n_p + n_s
    prow = lambda c: pl.BlockSpec((tile, c), lambda i: (jnp.minimum(i, ntp - 1), 0))
    srow = lambda c: pl.BlockSpec((tile, c), lambda i: (jnp.maximum(i - ntp, 0), 0))
    wa, wb = oap.shape[1], obp.shape[1]
    return pl.pallas_call(
        functools.partial(_merge_kernel, alpha=alpha, d=d, ne=ne, ntp=ntp),
        grid=(ntp + nts,),
        in_specs=[prow(d), srow(d), prow(wa), srow(wa), prow(wb), srow(wb)] + [_full(a.shape) for a in weights],
        out_specs=[pl.BlockSpec((tile, d), lambda i: (i, 0)),
                   pl.BlockSpec((tile * PACK_ROWS, LANES), lambda i: (i, 0)),
                   pl.BlockSpec((TOP_K, tile), lambda i: (0, i)),
                   pl.BlockSpec((TOP_K, tile), lambda i: (0, i))],
        out_shape=[jax.ShapeDtypeStruct((ntot, d), F32),
                   jax.ShapeDtypeStruct((ntot * PACK_ROWS, LANES), jnp.uint32),
                   jax.ShapeDtypeStruct((TOP_K, ntot), I32),
                   jax.ShapeDtypeStruct((TOP_K, ntot), F32)],
        compiler_params=_cparams(("arbitrary",)),
        name="merge",
    )(xp, xs, oap, oas, obp, obs, *weights)


def _plan_kernel(idx_ref, ltri_ref, su_ref, pos_ref, be_ref, bnv_ref, nused_ref, cnt_scr, base_scr,
                 *, ne, blk, nbp):
    ph = pl.program_id(0)
    i = pl.program_id(1)
    tm = idx_ref.shape[1]
    eio = lax.broadcasted_iota(I32, (ne, tm), 0)
    hits = [eio == idx_ref[k:k + 1, :] for k in range(TOP_K)]
    mh = jnp.zeros((ne, tm), F32)
    for hk in hits:
        mh = jnp.where(hk, 1.0, mh)
    rowsum = jnp.sum(mh, axis=1, keepdims=True)

    @pl.when((ph == 0) & (i == 0))
    def _():
        cnt_scr[...] = jnp.zeros(cnt_scr.shape, F32)

    @pl.when(ph == 0)
    def _():
        cnt_scr[...] = cnt_scr[...] + rowsum

    @pl.when((ph == 1) & (i == 0))
    def _():
        cnt = cnt_scr[...]
        sh = blk.bit_length() - 1
        nb = lax.shift_right_logical(cnt.astype(I32) + (blk - 1), sh)
        nbf = nb.astype(F32)
        start = jnp.dot(ltri_ref[...], jnp.broadcast_to(nbf, (ne, LANES)).astype(BF16),
                        preferred_element_type=F32)[:, 0:1]
        end = start + nbf
        base_scr[...] = start * float(blk)
        bio = lax.broadcasted_iota(I32, (ne, nbp), 1).astype(F32)
        be = jnp.sum(jnp.where(end <= bio, 1.0, 0.0), axis=0, keepdims=True)
        be_ref[...] = jnp.minimum(be, float(ne - 1)).astype(I32)
        inside = (start <= bio) & (bio < end)
        left = jnp.clip(cnt - (bio - start) * float(blk), 0.0, float(blk))
        bnv_ref[...] = jnp.sum(jnp.where(inside, left, 0.0), axis=0, keepdims=True).astype(I32)
        nused_ref[...] = jnp.broadcast_to(jnp.sum(nbf, axis=0, keepdims=True), (1, LANES)).astype(I32)

    @pl.when(ph == 1)
    def _():
        before = jnp.dot(mh.astype(BF16), su_ref[...], preferred_element_type=F32)
        rank = base_scr[...] + before
        for k in range(TOP_K):
            row = jnp.sum(jnp.where(hits[k], rank, 0.0), axis=0, keepdims=True).astype(I32)
            pos_ref[k:k + 1, :] = row * PACK_ROWS
        base_scr[...] = base_scr[...] + rowsum


def _plan(idx, *, ne, blk, nbp, tile):
    ntot = idx.shape[1]
    nt = ntot // tile
    ltri = jnp.tril(jnp.ones((ne, ne), F32), -1).astype(BF16)
    su = jnp.triu(jnp.ones((tile, tile), F32), 1).astype(BF16)
    kern = functools.partial(_plan_kernel, ne=ne, blk=blk, nbp=nbp)
    meta = lambda: pl.BlockSpec((1, nbp), lambda p, i: (0, 0))
    return pl.pallas_call(
        kern,
        grid=(2, nt),
        in_specs=[pl.BlockSpec((TOP_K, tile), lambda p, i: (0, i)), _full((ne, ne)), _full((tile, tile))],
        out_specs=[pl.BlockSpec((TOP_K, tile), lambda p, i: (0, i * p)), meta(), meta(),
                   pl.BlockSpec((1, LANES), lambda p, i: (0, 0))],
        out_shape=[jax.ShapeDtypeStruct((TOP_K, ntot), I32), jax.ShapeDtypeStruct((1, nbp), I32),
                   jax.ShapeDtypeStruct((1, nbp), I32), jax.ShapeDtypeStruct((1, LANES), I32)],
        scratch_shapes=[pltpu.VMEM((ne, 1), F32), pltpu.VMEM((ne, 1), F32)],
        compiler_params=_cparams(("arbitrary", "arbitrary")),
        name="plan",
    )(idx, ltri, su)


def _scatter_kernel(pos_ref, x_ref, xs_ref, sem):
    tm = pos_ref.shape[1]

    def tok(t, c):
        src = x_ref.at[pl.ds(pl.multiple_of(t * PACK_ROWS, PACK_ROWS), PACK_ROWS), :]
        for k in range(TOP_K):
            p = pl.multiple_of(pos_ref[k, t], PACK_ROWS)
            pltpu.make_async_copy(src, xs_ref.at[pl.ds(p, PACK_ROWS), :], sem).start(priority=k % 2)
        return c

    lax.fori_loop(0, tm, tok, 0)
    for _ in range(TOP_K):
        pltpu.make_async_copy(x_ref, xs_ref.at[pl.ds(0, tm * PACK_ROWS), :], sem).wait()


def _scatter(pos, x1p, *, rows, tile):
    ntot = pos.shape[1]
    return pl.pallas_call(
        _scatter_kernel,
        grid=(ntot // tile,),
        in_specs=[pl.BlockSpec((TOP_K, tile), lambda i: (0, i), memory_space=pltpu.SMEM),
                  pl.BlockSpec((tile * PACK_ROWS, LANES), lambda i: (i, 0))],
        out_specs=pl.BlockSpec(memory_space=pl.ANY),
        out_shape=jax.ShapeDtypeStruct((rows * PACK_ROWS, LANES), jnp.uint32),
        scratch_shapes=[pltpu.SemaphoreType.DMA],
        compiler_params=_cparams(("arbitrary",)),
        name="scatter",
    )(pos, x1p)


def _ffn_kernel(be_ref, bnv_ref, nused_ref, xs_ref, wg_ref, wu_ref, wd_ref, ys_ref, *, blk):
    b = pl.program_id(0)

    @pl.when(b < nused_ref[0])
    def _():
        rid = lax.broadcasted_iota(I32, (blk, 1), 0)
        x = jnp.where(rid < bnv_ref[b], _load_packed(xs_ref, blk), 0.0).astype(BF16)
        g = jnp.dot(x, wg_ref[0].astype(BF16), preferred_element_type=F32)
        u = jnp.dot(x, wu_ref[0].astype(BF16), preferred_element_type=F32)
        h = (g * jax.nn.sigmoid(g)) * u
        _store_packed(ys_ref, jnp.dot(h.astype(BF16), wd_ref[0].astype(BF16), preferred_element_type=F32))


def _ffn(be, bnv, nused, xs, wg, wu, wd, *, blk):
    d, de = wg.shape[1], wg.shape[2]
    nb = xs.shape[0] // (blk * PACK_ROWS)
    rows_map = lambda b, be, bnv, nu: (jnp.minimum(b, nu[0] - 1), 0)
    w_map = lambda b, be, bnv, nu: (be[jnp.minimum(b, nu[0] - 1)], 0, 0)
    grid_spec = pltpu.PrefetchScalarGridSpec(
        num_scalar_prefetch=3,
        grid=(nb,),
        in_specs=[pl.BlockSpec((blk * PACK_ROWS, LANES), rows_map),
                  pl.BlockSpec((1, d, de), w_map),
                  pl.BlockSpec((1, d, de), w_map),
                  pl.BlockSpec((1, de, d), w_map)],
        out_specs=pl.BlockSpec((blk * PACK_ROWS, LANES), rows_map),
    )
    return pl.pallas_call(
        functools.partial(_ffn_kernel, blk=blk),
        grid_spec=grid_spec,
        out_shape=jax.ShapeDtypeStruct(xs.shape, jnp.uint32),
        compiler_params=_cparams(("arbitrary",)),
        name="ffn",
    )(be, bnv, nused, xs, wg, wu, wd)


def _final_kernel(pos_ref, x1_ref, wts_ref, ys_ref, wsg_ref, wsu_ref, wsd_ref, g2_ref, b2_ref, y_ref,
                  gbuf, sem, *, alpha):
    tm = x1_ref.shape[0]

    def tok(t, c):
        dst = pl.multiple_of(t * PACK_ROWS, PACK_ROWS)
        for k in range(TOP_K):
            p = pl.multiple_of(pos_ref[k, t], PACK_ROWS)
            pltpu.make_async_copy(ys_ref.at[pl.ds(p, PACK_ROWS), :], gbuf.at[k, pl.ds(dst, PACK_ROWS), :],
                                  sem).start(priority=k % 2)
        return c

    lax.fori_loop(0, tm, tok, 0)
    x1 = x1_ref[...]
    xb = x1.astype(BF16)
    g = jnp.dot(xb, wsg_ref[...], preferred_element_type=F32)
    u = jnp.dot(xb, wsu_ref[...], preferred_element_type=F32)
    shared = jnp.dot(((g * jax.nn.sigmoid(g)) * u).astype(BF16), wsd_ref[...], preferred_element_type=F32)
    eye = lax.broadcasted_iota(I32, (tm, tm), 0) == lax.broadcasted_iota(I32, (tm, tm), 1)
    for k in range(TOP_K):
        pltpu.make_async_copy(ys_ref.at[pl.ds(0, tm * PACK_ROWS), :], gbuf.at[k], sem).wait()
    routed = jnp.zeros(x1.shape, F32)
    for k in range(TOP_K):
        wcol = jnp.sum(jnp.where(eye, wts_ref[k:k + 1, :], 0.0), axis=1, keepdims=True)
        routed = routed + _load_packed(gbuf.at[k], tm) * wcol
    z = alpha * x1 + (routed + shared)
    mu = jnp.mean(z, axis=-1, keepdims=True)
    var = jnp.mean(jnp.square(z - mu), axis=-1, keepdims=True)
    y_ref[...] = (z - mu) * lax.rsqrt(var + EPS) * g2_ref[...] + b2_ref[...]


def _final(pos, x1, wts, ys, wsg, wsu, wsd, g2, b2, *, tile, off, n, alpha):
    d = x1.shape[1]
    kern = functools.partial(_final_kernel, alpha=alpha)
    return pl.pallas_call(
        kern,
        grid=(n // tile,),
        in_specs=[pl.BlockSpec((TOP_K, tile), lambda i: (0, i + off), memory_space=pltpu.SMEM),
                  pl.BlockSpec((tile, d), lambda i: (i + off, 0)),
                  pl.BlockSpec((TOP_K, tile), lambda i: (0, i + off)),
                  pl.BlockSpec(memory_space=pl.ANY),
                  _full(wsg.shape), _full(wsu.shape), _full(wsd.shape), _full(g2.shape), _full(b2.shape)],
        out_specs=pl.BlockSpec((tile, d), lambda i: (i, 0)),
        out_shape=jax.ShapeDtypeStruct((n, d), F32),
        scratch_shapes=[pltpu.VMEM((TOP_K, tile * PACK_ROWS, LANES), jnp.uint32), pltpu.SemaphoreType.DMA],
        compiler_params=_cparams(("arbitrary",)),
        name="final",
    )(pos, x1, wts, ys, wsg, wsu, wsd, g2, b2)


def _rope_tables(pos, qk_rope, nh):
    half = qk_rope // 2
    inv = ROPE_BASE ** (-jnp.arange(half, dtype=F32) / half)
    ang = pos.astype(F32)[:, None] * inv
    cos, sin = jnp.cos(ang), jnp.sin(ang)
    c = jnp.concatenate([cos, cos], axis=1)
    s = jnp.concatenate([-sin, sin], axis=1)
    rep = LANES // qk_rope
    rope_k = jnp.concatenate([jnp.tile(c, (1, rep)), jnp.tile(s, (1, rep))], axis=1)
    rope_q = jnp.concatenate([jnp.tile(c, (1, nh)), jnp.tile(s, (1, nh))], axis=1)
    return rope_k, rope_q


def _swap_halves(w, axis):
    a, b = jnp.split(w, 2, axis=axis)
    return jnp.concatenate([b, a], axis=axis)


def _block_diag_pairs(w):
    nh, r, c = w.shape
    z = jnp.zeros((nh // 2, r, c), w.dtype)
    top = jnp.concatenate([w[0::2], z], axis=2)
    bot = jnp.concatenate([z, w[1::2]], axis=2)
    return jnp.concatenate([top, bot], axis=1)


def kernel(x_prompt, x_sample, cache_mla_latent, cache_mla_krope, cache_diff_k, cache_diff_v, w_in, b_gate, g_q_norm, w_uq, w_uk, g_kv_norm, w_uv, w_o_mla, lambda_q1, lambda_k1, lambda_q2, lambda_k2, g_subln, w_o_diff, w_out, g_ln1, b_ln1, w_router, router_bias, w_exp_gate, w_exp_up, w_exp_down, w_sh_gate, w_sh_up, w_sh_down, g_ln2, b_ln2):
    depth = w_in.shape[0]
    assert depth == 1
    b, s, d = x_prompt.shape
    bs, ss, _ = x_sample.shape
    past = cache_mla_latent.shape[2]
    q_lora = g_q_norm.shape[1]
    kv_lora = g_kv_norm.shape[1]
    qk_rope = cache_mla_krope.shape[3]
    nh = w_uq.shape[2]
    qk_nope = w_uq.shape[3] - qk_rope
    mla_v = w_uv.shape[3]
    dnh, _, hd = cache_diff_k.shape[3:]
    diff_w = dnh * 2 * hd
    ne = w_router.shape[2]
    alpha = (2 * depth) ** 0.25
    lam_init = 0.8 - 0.6 * math.exp(-0.3 * 0)
    assert 2 * hd == LANES and kv_lora == LANES and 2 * qk_nope == LANES and LANES % qk_rope == 0

    wi = w_in[0]
    c0 = q_lora + kv_lora
    w_cq, w_ckv = wi[:, :q_lora], wi[:, q_lora:c0]
    w_kr = wi[:, c0:c0 + qk_rope]
    c1 = c0 + qk_rope
    w_d = wi[:, c1:c1 + 3 * diff_w]
    w_gates = wi[:, c1 + 3 * diff_w:]
    rep = LANES // qk_rope
    w1 = jnp.concatenate([w_cq, w_ckv, w_d, jnp.tile(w_kr, (1, rep)),
                          jnp.tile(_swap_halves(w_kr, 1), (1, rep))], axis=1).astype(BF16)
    uq = w_uq[0]
    wqn = uq[:, :, :qk_nope].reshape(q_lora, nh * qk_nope).astype(BF16)
    wqr = uq[:, :, qk_nope:].reshape(q_lora, nh * qk_rope).astype(BF16)
    wqs = _swap_halves(uq[:, :, qk_nope:], 2).reshape(q_lora, nh * qk_rope).astype(BF16)
    wuk = _block_diag_pairs(jnp.transpose(w_uk[0], (1, 2, 0))).astype(BF16)
    wuv = _block_diag_pairs(jnp.transpose(w_uv[0], (1, 0, 2))).astype(BF16)
    lam4 = jnp.concatenate([lambda_q1, lambda_k1, lambda_q2, lambda_k2], axis=0)
    merge_w = (w_gates.astype(BF16), b_gate, w_o_mla[0].astype(BF16), w_o_diff[0].astype(BF16),
               w_out[0].astype(BF16), g_ln1, b_ln1, jnp.transpose(w_router[0]).astype(BF16),
               jnp.transpose(router_bias))
    dims = (q_lora, kv_lora, diff_w, qk_rope)
    mla_scale = (qk_nope + qk_rope) ** -0.5
    diff_scale = hd ** -0.5

    n_p = b * s
    rope_k, rope_q = _rope_tables(jnp.arange(s, dtype=I32), qk_rope, nh)
    cq, lat, kr, kcat, dq, dk, dkb, dv, dvb = _proj(
        x_prompt.reshape(n_p, d), w1, g_q_norm, g_kv_norm, rope_k, tile=PROJ_TILE, dims=dims)
    r3 = lambda a: a.reshape(b, s, a.shape[1])
    oa = _mla(r3(cq), rope_q, r3(kcat), wqn, wqr, wqs, wuk, wuv, qb=ATT_BLOCK, kb=ATT_BLOCK, nh=nh,
              qk_rope=qk_rope, past=0, n_valid=s, scale=mla_scale, causal=True)
    ob = _diff(r3(dq), r3(dkb), r3(dvb), lam4, g_subln, qb=ATT_BLOCK, kb=ATT_BLOCK, nh=dnh, hd=hd,
               past=0, n_valid=s, scale=diff_scale, causal=True, lam_init=lam_init)

    n_s = bs * ss
    sk = past + ss
    skp = -(-sk // LANES) * LANES
    rope_ks, rope_qs = _rope_tables(past + jnp.arange(ss, dtype=I32), qk_rope, nh)
    cq_s, lat_s, kr_s, kcat_s, dq_s, dk_s, dkb_s, dv_s, dvb_s = _proj(
        x_sample.reshape(n_s, d), w1, g_q_norm, g_kv_norm, jnp.tile(rope_ks, (bs, 1)), tile=n_s, dims=dims)
    r3s = lambda a: a.reshape(bs, ss, a.shape[1])
    padk = lambda a: jnp.pad(a, ((0, 0), (0, skp - sk), (0, 0)))
    kcat_all = padk(jnp.concatenate(
        [jnp.concatenate([cache_mla_latent[0], jnp.tile(cache_mla_krope[0], (1, 1, rep))], axis=2).astype(BF16),
         r3s(kcat_s)], axis=1))
    dk_all = padk(jnp.concatenate([cache_diff_k[0].reshape(bs, past, diff_w).astype(BF16), r3s(dkb_s)], axis=1))
    dv_all = padk(jnp.concatenate([cache_diff_v[0].reshape(bs, past, diff_w).astype(BF16), r3s(dvb_s)], axis=1))
    oa_s = _mla(r3s(cq_s), rope_qs, kcat_all, wqn, wqr, wqs, wuk, wuv, qb=ss, kb=skp, nh=nh,
                qk_rope=qk_rope, past=past, n_valid=sk, scale=mla_scale, causal=False)
    ob_s = _diff(r3s(dq_s), dk_all, dv_all, lam4, g_subln, qb=ss, kb=skp, nh=dnh, hd=hd,
                 past=past, n_valid=sk, scale=diff_scale, causal=False, lam_init=lam_init)

    ntot = n_p + n_s
    x1, x1p, idx, wts = _merge(x_prompt.reshape(n_p, d), x_sample.reshape(n_s, d), oa.reshape(n_p, -1),
                          oa_s.reshape(n_s, -1), ob.reshape(n_p, -1), ob_s.reshape(n_s, -1), merge_w,
                          tile=TOK_TILE, alpha=alpha, ne=ne)
    nblocks = -(-(ntot * TOP_K) // EXP_BLOCK) + ne
    nbp = -(-nblocks // LANES) * LANES
    pos, be, bnv, nused = _plan(idx, ne=ne, blk=EXP_BLOCK, nbp=nbp, tile=TOK_TILE)
    xs = _scatter(pos, x1p, rows=nblocks * EXP_BLOCK, tile=TOK_TILE)
    ys = _ffn(be[0, :nblocks], bnv[0, :nblocks], nused[0, :1], xs, w_exp_gate[0], w_exp_up[0], w_exp_down[0],
              blk=EXP_BLOCK)
    fin_w = (w_sh_gate[0].astype(BF16), w_sh_up[0].astype(BF16), w_sh_down[0].astype(BF16), g_ln2, b_ln2)
    y_p = _final(pos, x1, wts, ys, *fin_w, tile=TOK_TILE, off=0, n=n_p, alpha=alpha)
    y_s = _final(pos, x1, wts, ys, *fin_w, tile=TOK_TILE, off=n_p // TOK_TILE, n=n_s, alpha=alpha)

    st = lambda a, bb, sq, tail: a.reshape((1, bb, sq) + tail)
    return (y_p.reshape(b, s, d), y_s.reshape(bs, ss, d),
            st(lat, b, s, (kv_lora,)), st(kr, b, s, (qk_rope,)),
            st(dk, b, s, (dnh, 2, hd)), st(dv, b, s, (dnh, 2 * hd)),
            st(lat_s, bs, ss, (kv_lora,)), st(kr_s, bs, ss, (qk_rope,)),
            st(dk_s, bs, ss, (dnh, 2, hd)), st(dv_s, bs, ss, (dnh, 2 * hd)))
```

```python
import functools
import math

import jax
import jax.numpy as jnp
from jax import lax
from jax.experimental import pallas as pl
from jax.experimental.pallas import tpu as pltpu

F32 = jnp.float32
BF16 = jnp.bfloat16
I32 = jnp.int32

CHUNK = 64
ROPE_BASE = 10000.0
EPS = 1e-6
N_GROUPS = 8
TOPK_GROUPS = 4
TOP_K = 8
ROUTED_SCALE = 2.5
NEG = -1e30

LANES = 128
TOK_TILE = 256
PROJ_TILE = 512
ATT_BLOCK = 512
SLAB_ROWS = 256
EXP_BLOCK = 512
VMEM_LIMIT = 56 * 1024 * 1024


def _cparams(sem):
    return pltpu.CompilerParams(dimension_semantics=sem, vmem_limit_bytes=VMEM_LIMIT)


def _full(shape):
    nd = len(shape)
    return pl.BlockSpec(shape, lambda *_: (0,) * nd)


def _proj_kernel(x_ref, w_ref, gq_ref, gkv_ref, rope_ref,
                 cq_ref, lat_ref, kr_ref, kcat_ref, dq_ref, dk_ref, dkb_ref, dv_ref, dvb_ref,
                 *, q_lora, kv_lora, diff_w, qk_rope, dq_scale):
    x = x_ref[...].astype(BF16)
    h = jnp.dot(x, w_ref[...], preferred_element_type=F32)
    o = 0
    cq = h[:, o:o + q_lora]
    o += q_lora
    cq = cq * lax.rsqrt(jnp.mean(cq * cq, axis=-1, keepdims=True) + EPS) * gq_ref[...]
    cq_ref[...] = cq.astype(BF16)
    ckv = h[:, o:o + kv_lora]
    o += kv_lora
    lat = ckv * lax.rsqrt(jnp.mean(ckv * ckv, axis=-1, keepdims=True) + EPS) * gkv_ref[...]
    lat_ref[...] = lat
    dq_ref[...] = (h[:, o:o + diff_w] * dq_scale).astype(BF16)
    o += diff_w
    dk = h[:, o:o + diff_w]
    o += diff_w
    dk_ref[...] = dk
    dkb_ref[...] = dk.astype(BF16)
    dv = h[:, o:o + diff_w]
    o += diff_w
    dv_ref[...] = dv
    dvb_ref[...] = dv.astype(BF16)
    krr = h[:, o:o + LANES] * rope_ref[:, 0:LANES] + h[:, o + LANES:o + 2 * LANES] * rope_ref[:, LANES:2 * LANES]
    kr_ref[...] = krr[:, 0:qk_rope]
    kcat_ref[...] = jnp.concatenate([lat, krr], axis=1).astype(BF16)


def _proj(x, w1, gq, gkv, rope_k, *, tile, dims):
    n, d = x.shape
    q_lora, kv_lora, diff_w, qk_rope, dq_scale = dims
    wcols = w1.shape[1]
    nrope = rope_k.shape[0] // tile
    row = lambda c: pl.BlockSpec((tile, c), lambda i: (i, 0))
    outs = [
        jax.ShapeDtypeStruct((n, q_lora), BF16),
        jax.ShapeDtypeStruct((n, kv_lora), F32),
        jax.ShapeDtypeStruct((n, qk_rope), F32),
        jax.ShapeDtypeStruct((n, 2 * LANES), BF16),
        jax.ShapeDtypeStruct((n, diff_w), BF16),
        jax.ShapeDtypeStruct((n, diff_w), F32),
        jax.ShapeDtypeStruct((n, diff_w), BF16),
        jax.ShapeDtypeStruct((n, diff_w), F32),
        jax.ShapeDtypeStruct((n, diff_w), BF16),
    ]
    return pl.pallas_call(
        functools.partial(_proj_kernel, q_lora=q_lora, kv_lora=kv_lora, diff_w=diff_w, qk_rope=qk_rope,
                          dq_scale=dq_scale),
        grid=(n // tile,),
        in_specs=[row(d), _full((d, wcols)), _full((1, q_lora)), _full((1, kv_lora)),
                  pl.BlockSpec((tile, 2 * LANES), lambda i: (i % nrope, 0))],
        out_specs=[row(q_lora), row(kv_lora), row(qk_rope), row(2 * LANES), row(diff_w), row(diff_w),
                   row(diff_w), row(diff_w), row(diff_w)],
        out_shape=outs,
        compiler_params=_cparams(("arbitrary",)),
        name="proj",
    )(x, w1, gq, gkv, rope_k)


LOG2E = math.log2(math.e)


def _softmax_step(s2, v, m_scr, accl_scr, r0, rows, kb):
    m_prev = m_scr[r0:r0 + rows, :]
    m_cur = jnp.max(s2, axis=1, keepdims=True).astype(BF16).astype(F32)
    m_next = jnp.maximum(m_prev, m_cur)
    mb = m_next.astype(BF16)
    p = jnp.exp2(s2.astype(BF16) - jnp.concatenate([mb] * (kb // LANES), axis=1))
    alpha = jnp.exp2(m_prev - m_next)
    m_scr[r0:r0 + rows, :] = m_next
    v1 = jnp.concatenate([v, jnp.ones(v.shape, BF16)], axis=1)
    pv = jnp.dot(p, v1, preferred_element_type=F32)
    accl_scr[r0:r0 + rows, :] = accl_scr[r0:r0 + rows, :] * jnp.concatenate([alpha, alpha], axis=1) + pv


def _visible(qb, kb, q0, k0, n_valid):
    qpos = q0 + lax.broadcasted_iota(I32, (qb, kb), 0)
    kpos = k0 + lax.broadcasted_iota(I32, (qb, kb), 1)
    sh = CHUNK.bit_length() - 1
    vis = (lax.shift_right_logical(kpos, sh) <= lax.shift_right_logical(qpos, sh)) & (kpos < n_valid)
    return vis, qpos, kpos


def _slab_rows(x, r0, rp, qb):
    if rp >= qb:
        return x[None]
    q0 = r0 % qb
    return x[q0:q0 + rp]


def _mask_slab(s2, vis, r0, rp, qb, w):
    if rp >= qb:
        return jnp.where(vis[None], s2.reshape(rp // qb, qb, w), NEG).reshape(rp, w)
    return jnp.where(_slab_rows(vis, r0, rp, qb), s2, NEG)


def _key_blocks(i, kb, causal, block):
    if not causal:
        block(0, kb, True)
        return

    def body(j, c):
        block(pl.multiple_of(j * kb, kb), kb, False)
        return c

    lax.fori_loop(0, i, body, 0)
    block(pl.multiple_of(i * kb, kb), kb, True)


def _mla_kernel(cq_ref, ropeq_ref, kcat_ref, wqn_ref, wqr_ref, wqs_ref, wuk_ref, wuv_ref, o_ref,
                qcat_scr, m_scr, accl_scr, *, qb, kb, nh, qk_rope, past, n_valid, scale, causal):
    i = pl.program_id(1)
    rows = nh * qb
    c2 = scale * LOG2E
    cq = cq_ref[0]
    qn = jnp.dot(cq, wqn_ref[...], preferred_element_type=F32).astype(BF16)
    qr = jnp.dot(cq, wqr_ref[...], preferred_element_type=F32)
    qs = jnp.dot(cq, wqs_ref[...], preferred_element_type=F32)
    rw = nh * qk_rope
    rot = (qr * ropeq_ref[:, 0:rw] + qs * ropeq_ref[:, rw:2 * rw]) * c2
    lane = lax.broadcasted_iota(I32, (qb, LANES), 1)
    per = LANES // qk_rope
    for j in range(nh // 2):
        ql = jnp.dot(qn[:, LANES * j:LANES * (j + 1)], wuk_ref[j], preferred_element_type=F32) * c2
        for u in range(2):
            hh = 2 * j + u
            rblk = rot[:, LANES * (hh // per):LANES * (hh // per + 1)]
            lo = qk_rope * (hh % per)
            part2 = jnp.where((lane >= lo) & (lane < lo + qk_rope), rblk, 0.0)
            qcat_scr[hh * qb:(hh + 1) * qb, :] = jnp.concatenate(
                [ql[:, LANES * u:LANES * (u + 1)], part2], axis=1).astype(BF16)
    m_scr[...] = jnp.full(m_scr.shape, NEG, F32)
    accl_scr[...] = jnp.zeros(accl_scr.shape, F32)

    def block(k0, w, masked):
        kblk = kcat_ref[0, pl.ds(k0, w), :]
        if masked:
            vis, _, _ = _visible(qb, w, past + i * qb, k0, n_valid)
        rp = min(rows, SLAB_ROWS)
        for sl in range(rows // rp):
            r0 = sl * rp
            s2 = lax.dot_general(qcat_scr[r0:r0 + rp, :], kblk, (((1,), (1,)), ((), ())),
                                 preferred_element_type=F32)
            if masked:
                s2 = _mask_slab(s2, vis, r0, rp, qb, w)
            _softmax_step(s2, kblk[:, 0:LANES], m_scr, accl_scr, r0, rp, w)

    _key_blocks(i, kb, causal, block)

    o_lat = accl_scr[:, 0:LANES] / accl_scr[:, LANES:2 * LANES]
    for j in range(nh // 2):
        pair = jnp.concatenate([o_lat[(2 * j) * qb:(2 * j + 1) * qb, :],
                                o_lat[(2 * j + 1) * qb:(2 * j + 2) * qb, :]], axis=1).astype(BF16)
        o_ref[0, :, LANES * j:LANES * (j + 1)] = jnp.dot(
            pair, wuv_ref[j], preferred_element_type=F32).astype(BF16)


def _mla(cq, rope_q, kcat, wqn, wqr, wqs, wuk, wuv, *, qb, kb, nh, qk_rope, past, n_valid, scale, causal):
    b, sq, q_lora = cq.shape
    sk = kcat.shape[1]
    nq = sq // qb
    rows = nh * qb
    ov = wuv.shape[0] * wuv.shape[2]
    kern = functools.partial(_mla_kernel, qb=qb, kb=kb, nh=nh, qk_rope=qk_rope, past=past,
                             n_valid=n_valid, scale=scale, causal=causal)
    return pl.pallas_call(
        kern,
        grid=(b, nq),
        in_specs=[pl.BlockSpec((1, qb, q_lora), lambda bb, i: (bb, i, 0)),
                  pl.BlockSpec((qb, rope_q.shape[1]), lambda bb, i: (i, 0)),
                  pl.BlockSpec((1, sk, kcat.shape[2]), lambda bb, i: (bb, 0, 0)),
                  _full(wqn.shape), _full(wqr.shape), _full(wqs.shape), _full(wuk.shape), _full(wuv.shape)],
        out_specs=pl.BlockSpec((1, qb, ov), lambda bb, i: (bb, i, 0)),
        out_shape=jax.ShapeDtypeStruct((b, sq, ov), BF16),
        scratch_shapes=[pltpu.VMEM((rows, 2 * LANES), BF16), pltpu.VMEM((rows, LANES), F32),
                        pltpu.VMEM((rows, 2 * LANES), F32)],
        compiler_params=_cparams(("arbitrary", "arbitrary")),
        name="mla",
    )(cq, rope_q, kcat, wqn, wqr, wqs, wuk, wuv)


def _diff_kernel(dq_ref, dk_ref, dv_ref, lam_ref, gsub_ref, o_ref, qq_scr, m_scr, accl_scr,
                 *, qb, kb, nh, hd, past, n_valid, causal, lam_init):
    i = pl.program_id(1)
    q = dq_ref[0]
    lane = lax.broadcasted_iota(I32, (qb, LANES), 1)
    zero = jnp.zeros((), BF16)
    for h in range(nh):
        qh = q[:, LANES * h:LANES * (h + 1)]
        qq_scr[(2 * h) * qb:(2 * h + 1) * qb, :] = jnp.where(lane < hd, qh, zero)
        qq_scr[(2 * h + 1) * qb:(2 * h + 2) * qb, :] = jnp.where(lane >= hd, qh, zero)
    m_scr[...] = jnp.full(m_scr.shape, NEG, F32)
    accl_scr[...] = jnp.zeros(accl_scr.shape, F32)

    def block(k0, w, masked):
        vis, qpos, kpos = _visible(qb, w, past + i * qb, k0, n_valid)
        dist = jnp.abs(qpos - kpos).astype(F32)
        for h in range(nh):
            slope2 = 2.0 ** (-8.0 * (h + 1) / nh) * LOG2E
            kh = dk_ref[0, pl.ds(k0, w), LANES * h:LANES * (h + 1)]
            vh = dv_ref[0, pl.ds(k0, w), LANES * h:LANES * (h + 1)]
            bias = slope2 * dist
            rp = min(2 * qb, SLAB_ROWS)
            for sl in range(2 * qb // rp):
                r0 = 2 * h * qb + sl * rp
                s2 = lax.dot_general(qq_scr[r0:r0 + rp, :], kh, (((1,), (1,)), ((), ())),
                                     preferred_element_type=F32)
                if rp >= qb:
                    s2 = (s2.reshape(rp // qb, qb, w) - bias[None]).reshape(rp, w)
                else:
                    s2 = s2 - _slab_rows(bias, r0, rp, qb)
                if masked:
                    s2 = _mask_slab(s2, vis, r0, rp, qb, w)
                _softmax_step(s2, vh, m_scr, accl_scr, r0, rp, w)

    _key_blocks(i, kb, causal, block)

    lam1 = jnp.sum(lam_ref[0:1, :] * lam_ref[1:2, :], axis=1, keepdims=True)
    lam2 = jnp.sum(lam_ref[2:3, :] * lam_ref[3:4, :], axis=1, keepdims=True)
    lam = jnp.exp(lam1) - jnp.exp(lam2) + lam_init
    o_all = accl_scr[:, 0:LANES] / accl_scr[:, LANES:2 * LANES]
    for h in range(nh):
        o = o_all[(2 * h) * qb:(2 * h + 1) * qb, :] - lam * o_all[(2 * h + 1) * qb:(2 * h + 2) * qb, :]
        o = o * lax.rsqrt(jnp.mean(o * o, axis=-1, keepdims=True) + EPS) * gsub_ref[...]
        o_ref[0, :, LANES * h:LANES * (h + 1)] = (o * (1.0 - lam_init)).astype(BF16)


def _diff(dq, dk, dv, lam4, gsub, *, qb, kb, nh, hd, past, n_valid, causal, lam_init):
    b, sq, w = dq.shape
    sk = dk.shape[1]
    nq = sq // qb
    rows = 2 * nh * qb
    kern = functools.partial(_diff_kernel, qb=qb, kb=kb, nh=nh, hd=hd, past=past, n_valid=n_valid,
                             causal=causal, lam_init=lam_init)
    return pl.pallas_call(
        kern,
        grid=(b, nq),
        in_specs=[pl.BlockSpec((1, qb, w), lambda bb, i: (bb, i, 0)),
                  pl.BlockSpec((1, sk, w), lambda bb, i: (bb, 0, 0)),
                  pl.BlockSpec((1, sk, w), lambda bb, i: (bb, 0, 0)),
                  _full(lam4.shape), _full(gsub.shape)],
        out_specs=pl.BlockSpec((1, qb, w), lambda bb, i: (bb, i, 0)),
        out_shape=jax.ShapeDtypeStruct((b, sq, w), BF16),
        scratch_shapes=[pltpu.VMEM((rows, LANES), BF16), pltpu.VMEM((rows, LANES), F32),
                        pltpu.VMEM((rows, 2 * LANES), F32)],
        compiler_params=_cparams(("arbitrary", "arbitrary")),
        name="diffattn",
    )(dq, dk, dv, lam4, gsub)


def _red2(x, fn):
    return fn(fn(x, axis=0, keepdims=True), axis=1, keepdims=True)


PACK_ROWS = 4


def _store_packed(ref, x):
    rows, d = x.shape
    bits = pltpu.bitcast(x.astype(BF16).astype(F32), jnp.uint32)
    packed = bits[:, 0:d // 2] | (bits[:, d // 2:d] >> 16)
    for j in range(PACK_ROWS):
        ref[pl.ds(j, rows, stride=PACK_ROWS), :] = packed[:, LANES * j:LANES * (j + 1)]


def _load_packed(ref, rows):
    words = [ref[pl.ds(j, rows, stride=PACK_ROWS), :] for j in range(PACK_ROWS)]
    hi = [pltpu.bitcast(w & jnp.uint32(0xFFFF0000), F32) for w in words]
    lo = [pltpu.bitcast(w << 16, F32) for w in words]
    return jnp.concatenate(hi + lo, axis=1)


def _merge_kernel(xp_ref, xs_ref, oap_ref, oas_ref, obp_ref, obs_ref, wg_ref, bg_ref, woa_ref, wob_ref,
                  wout_ref, g1_ref, b1_ref, wrt_ref, rb_ref, x1_ref, x1p_ref, idx_ref, wts_ref, *, alpha, d, ne, ntp):
    is_p = pl.program_id(0) < ntp
    x = jnp.where(is_p, xp_ref[...], xs_ref[...])
    oa = jnp.where(is_p, oap_ref[...], oas_ref[...])
    ob = jnp.where(is_p, obp_ref[...], obs_ref[...])
    gates = jnp.dot(x.astype(BF16), wg_ref[...], preferred_element_type=F32) + bg_ref[...]
    g = jax.nn.sigmoid(gates)
    br_a = jnp.dot(oa, woa_ref[...], preferred_element_type=F32)
    br_b = jnp.dot(ob, wob_ref[...], preferred_element_type=F32)
    mixin = g[:, 0:d] * br_a + g[:, d:2 * d] * br_b
    mix = jnp.dot(mixin.astype(BF16), wout_ref[...], preferred_element_type=F32)
    z = alpha * x + mix
    mu = jnp.mean(z, axis=-1, keepdims=True)
    var = jnp.mean(jnp.square(z - mu), axis=-1, keepdims=True)
    x1 = (z - mu) * lax.rsqrt(var + EPS) * g1_ref[...] + b1_ref[...]
    x1_ref[...] = x1
    _store_packed(x1p_ref, x1)

    tm = x.shape[0]
    gsz = ne // N_GROUPS
    logits = lax.dot_general(wrt_ref[...], x1.astype(BF16), (((1,), (1,)), ((), ())),
                             preferred_element_type=F32)
    scores = jax.nn.sigmoid(logits)
    choice = scores + rb_ref[...]
    c3 = choice.reshape(N_GROUPS, gsz, tm)
    s3 = scores.reshape(N_GROUPS, gsz, tm)
    io = lax.broadcasted_iota(I32, (N_GROUPS, gsz, tm), 1)
    gio = lax.broadcasted_iota(I32, (N_GROUPS, gsz, tm), 0)
    eio = gio * gsz + io
    ninf = -jnp.inf
    m1 = jnp.max(c3, axis=1, keepdims=True)
    i1 = jnp.min(jnp.where(c3 == m1, io, gsz), axis=1, keepdims=True)
    m2 = jnp.max(jnp.where(io == i1, ninf, c3), axis=1, keepdims=True)
    gs = m1 + m2
    gio1 = lax.broadcasted_iota(I32, (N_GROUPS, 1, tm), 0)
    gsel = jnp.zeros((N_GROUPS, 1, tm), F32)
    cur = gs
    for _ in range(TOPK_GROUPS):
        gm = jnp.max(cur, axis=0, keepdims=True)
        gi = jnp.min(jnp.where(cur == gm, gio1, N_GROUPS), axis=0, keepdims=True)
        hit = gio1 == gi
        gsel = jnp.where(hit, 1.0, gsel)
        cur = jnp.where(hit, ninf, cur)
    cur = jnp.where(gsel > 0.5, c3, ninf)
    idx_rows, w_rows = [], []
    for _ in range(TOP_K):
        m = _red2(cur, jnp.max)
        ik = _red2(jnp.where(cur == m, eio, ne), jnp.min)
        hit = eio == ik
        w_rows.append(_red2(jnp.where(hit, s3, 0.0), jnp.sum)[0])
        idx_rows.append(ik[0])
        cur = jnp.where(hit, ninf, cur)
    wsum = w_rows[0]
    for w in w_rows[1:]:
        wsum = wsum + w
    for k in range(TOP_K):
        idx_ref[k:k + 1, :] = idx_rows[k]
        wts_ref[k:k + 1, :] = w_rows[k] / wsum * ROUTED_SCALE


def _merge(xp, xs, oap, oas, obp, obs, weights, *, tile, alpha, ne):
    n_p, d = xp.shape
    n_s = xs.shape[0]
    ntp, nts = n_p // tile, n_s // tile
    ntot = n_p + n_s
    prow = lambda c: pl.BlockSpec((tile, c), lambda i: (jnp.minimum(i, ntp - 1), 0))
    srow = lambda c: pl.BlockSpec((tile, c), lambda i: (jnp.maximum(i - ntp, 0), 0))
    wa, wb = oap.shape[1], obp.shape[1]
    return pl.pallas_call(
        functools.partial(_merge_kernel, alpha=alpha, d=d, ne=ne, ntp=ntp),
        grid=(ntp + nts,),
        in_specs=[prow(d), srow(d), prow(wa), srow(wa), prow(wb), srow(wb)] + [_full(a.shape) for a in weights],
        out_specs=[pl.BlockSpec((tile, d), lambda i: (i, 0)),
                   pl.BlockSpec((tile * PACK_ROWS, LANES), lambda i: (i, 0)),
                   pl.BlockSpec((TOP_K, tile), lambda i: (0, i)),
                   pl.BlockSpec((TOP_K, tile), lambda i: (0, i))],
        out_shape=[jax.ShapeDtypeStruct((ntot, d), F32),
                   jax.ShapeDtypeStruct((ntot * PACK_ROWS, LANES), jnp.uint32),
                   jax.ShapeDtypeStruct((TOP_K, ntot), I32),
                   jax.ShapeDtypeStruct((TOP_K, ntot), F32)],
        compiler_params=_cparams(("arbitrary",)),
        name="merge",
    )(xp, xs, oap, oas, obp, obs, *weights)


def _plan_kernel(idx_ref, ltri_ref, su_ref, pos_ref, be_ref, bnv_ref, nused_ref, cnt_scr, base_scr,
                 *, ne, blk, nbp):
    ph = pl.program_id(0)
    i = pl.program_id(1)
    tm = idx_ref.shape[1]
    eio = lax.broadcasted_iota(I32, (ne, tm), 0)
    hits = [eio == idx_ref[k:k + 1, :] for k in range(TOP_K)]
    mh = jnp.zeros((ne, tm), F32)
    for hk in hits:
        mh = jnp.where(hk, 1.0, mh)
    rowsum = jnp.sum(mh, axis=1, keepdims=True)

    @pl.when((ph == 0) & (i == 0))
    def _():
        cnt_scr[...] = jnp.zeros(cnt_scr.shape, F32)

    @pl.when(ph == 0)
    def _():
        cnt_scr[...] = cnt_scr[...] + rowsum

    @pl.when((ph == 1) & (i == 0))
    def _():
        cnt = cnt_scr[...]
        sh = blk.bit_length() - 1
        nb = lax.shift_right_logical(cnt.astype(I32) + (blk - 1), sh)
        nbf = nb.astype(F32)
        start = jnp.dot(ltri_ref[...], jnp.broadcast_to(nbf, (ne, LANES)).astype(BF16),
                        preferred_element_type=F32)[:, 0:1]
        end = start + nbf
        base_scr[...] = start * float(blk)
        bio = lax.broadcasted_iota(I32, (ne, nbp), 1).astype(F32)
        be = jnp.sum(jnp.where(end <= bio, 1.0, 0.0), axis=0, keepdims=True)
        be_ref[...] = jnp.minimum(be, float(ne - 1)).astype(I32)
        inside = (start <= bio) & (bio < end)
        left = jnp.clip(cnt - (bio - start) * float(blk), 0.0, float(blk))
        bnv_ref[...] = jnp.sum(jnp.where(inside, left, 0.0), axis=0, keepdims=True).astype(I32)
        nused_ref[...] = jnp.broadcast_to(jnp.sum(nbf, axis=0, keepdims=True), (1, LANES)).astype(I32)

    @pl.when(ph == 1)
    def _():
        before = jnp.dot(mh.astype(BF16), su_ref[...], preferred_element_type=F32)
        rank = base_scr[...] + before
        for k in range(TOP_K):
            row = jnp.sum(jnp.where(hits[k], rank, 0.0), axis=0, keepdims=True).astype(I32)
            pos_ref[k:k + 1, :] = row * PACK_ROWS
        base_scr[...] = base_scr[...] + rowsum


def _plan(idx, *, ne, blk, nbp, tile):
    ntot = idx.shape[1]
    nt = ntot // tile
    ltri = jnp.tril(jnp.ones((ne, ne), F32), -1).astype(BF16)
    su = jnp.triu(jnp.ones((tile, tile), F32), 1).astype(BF16)
    kern = functools.partial(_plan_kernel, ne=ne, blk=blk, nbp=nbp)
    meta = lambda: pl.BlockSpec((1, nbp), lambda p, i: (0, 0))
    return pl.pallas_call(
        kern,
        grid=(2, nt),
        in_specs=[pl.BlockSpec((TOP_K, tile), lambda p, i: (0, i)), _full((ne, ne)), _full((tile, tile))],
        out_specs=[pl.BlockSpec((TOP_K, tile), lambda p, i: (0, i * p)), meta(), meta(),
                   pl.BlockSpec((1, LANES), lambda p, i: (0, 0))],
        out_shape=[jax.ShapeDtypeStruct((TOP_K, ntot), I32), jax.ShapeDtypeStruct((1, nbp), I32),
                   jax.ShapeDtypeStruct((1, nbp), I32), jax.ShapeDtypeStruct((1, LANES), I32)],
        scratch_shapes=[pltpu.VMEM((ne, 1), F32), pltpu.VMEM((ne, 1), F32)],
        compiler_params=_cparams(("arbitrary", "arbitrary")),
        name="plan",
    )(idx, ltri, su)


def _scatter_kernel(pos_ref, x_ref, xs_ref, sem):
    tm = pos_ref.shape[1]

    def tok(t, c):
        src = x_ref.at[pl.ds(pl.multiple_of(t * PACK_ROWS, PACK_ROWS), PACK_ROWS), :]
        for k in range(TOP_K):
            p = pl.multiple_of(pos_ref[k, t], PACK_ROWS)
            pltpu.make_async_copy(src, xs_ref.at[pl.ds(p, PACK_ROWS), :], sem).start(priority=k % 2)
        return c

    lax.fori_loop(0, tm, tok, 0)
    for _ in range(TOP_K):
        pltpu.make_async_copy(x_ref, xs_ref.at[pl.ds(0, tm * PACK_ROWS), :], sem).wait()


def _scatter(pos, x1p, *, rows, tile):
    ntot = pos.shape[1]
    return pl.pallas_call(
        _scatter_kernel,
        grid=(ntot // tile,),
        in_specs=[pl.BlockSpec((TOP_K, tile), lambda i: (0, i), memory_space=pltpu.SMEM),
                  pl.BlockSpec((tile * PACK_ROWS, LANES), lambda i: (i, 0))],
        out_specs=pl.BlockSpec(memory_space=pl.ANY),
        out_shape=jax.ShapeDtypeStruct((rows * PACK_ROWS, LANES), jnp.uint32),
        scratch_shapes=[pltpu.SemaphoreType.DMA],
        compiler_params=_cparams(("arbitrary",)),
        name="scatter",
    )(pos, x1p)


def _ffn_kernel(be_ref, bnv_ref, nused_ref, xs_ref, wg_ref, wu_ref, wd_ref, ys_ref, *, blk):
    b = pl.program_id(0)

    @pl.when(b < nused_ref[0])
    def _():
        rid = lax.broadcasted_iota(I32, (blk, 1), 0)
        x = jnp.where(rid < bnv_ref[b], _load_packed(xs_ref, blk), 0.0).astype(BF16)
        g = jnp.dot(x, wg_ref[0].astype(BF16), preferred_element_type=F32)
        u = jnp.dot(x, wu_ref[0].astype(BF16), preferred_element_type=F32)
        h = (g * jax.nn.sigmoid(g)) * u
        _store_packed(ys_ref, jnp.dot(h.astype(BF16), wd_ref[0].astype(BF16), preferred_element_type=F32))


def _ffn(be, bnv, nused, xs, wg, wu, wd, *, blk):
    d, de = wg.shape[1], wg.shape[2]
    nb = xs.shape[0] // (blk * PACK_ROWS)
    rows_map = lambda b, be, bnv, nu: (jnp.minimum(b, nu[0] - 1), 0)
    w_map = lambda b, be, bnv, nu: (be[jnp.minimum(b, nu[0] - 1)], 0, 0)
    grid_spec = pltpu.PrefetchScalarGridSpec(
        num_scalar_prefetch=3,
        grid=(nb,),
        in_specs=[pl.BlockSpec((blk * PACK_ROWS, LANES), rows_map),
                  pl.BlockSpec((1, d, de), w_map),
                  pl.BlockSpec((1, d, de), w_map),
                  pl.BlockSpec((1, de, d), w_map)],
        out_specs=pl.BlockSpec((blk * PACK_ROWS, LANES), rows_map),
    )
    return pl.pallas_call(
        functools.partial(_ffn_kernel, blk=blk),
        grid_spec=grid_spec,
        out_shape=jax.ShapeDtypeStruct(xs.shape, jnp.uint32),
        compiler_params=_cparams(("arbitrary",)),
        name="ffn",
    )(be, bnv, nused, xs, wg, wu, wd)


def _final_kernel(pos_ref, x1_ref, wts_ref, ys_ref, wsg_ref, wsu_ref, wsd_ref, g2_ref, b2_ref, y_ref,
                  gbuf, sem, *, alpha):
    tm = x1_ref.shape[0]

    def tok(t, c):
        dst = pl.multiple_of(t * PACK_ROWS, PACK_ROWS)
        for k in range(TOP_K):
            p = pl.multiple_of(pos_ref[k, t], PACK_ROWS)
            pltpu.make_async_copy(ys_ref.at[pl.ds(p, PACK_ROWS), :], gbuf.at[k, pl.ds(dst, PACK_ROWS), :],
                                  sem).start(priority=k % 2)
        return c

    lax.fori_loop(0, tm, tok, 0)
    x1 = x1_ref[...]
    xb = x1.astype(BF16)
    g = jnp.dot(xb, wsg_ref[...], preferred_element_type=F32)
    u = jnp.dot(xb, wsu_ref[...], preferred_element_type=F32)
    shared = jnp.dot(((g * jax.nn.sigmoid(g)) * u).astype(BF16), wsd_ref[...], preferred_element_type=F32)
    eye = lax.broadcasted_iota(I32, (tm, tm), 0) == lax.broadcasted_iota(I32, (tm, tm), 1)
    for k in range(TOP_K):
        pltpu.make_async_copy(ys_ref.at[pl.ds(0, tm * PACK_ROWS), :], gbuf.at[k], sem).wait()
    routed = jnp.zeros(x1.shape, F32)
    for k in range(TOP_K):
        wcol = jnp.sum(jnp.where(eye, wts_ref[k:k + 1, :], 0.0), axis=1, keepdims=True)
        routed = routed + _load_packed(gbuf.at[k], tm) * wcol
    z = alpha * x1 + (routed + shared)
    mu = jnp.mean(z, axis=-1, keepdims=True)
    var = jnp.mean(jnp.square(z - mu), axis=-1, keepdims=True)
    y_ref[...] = (z - mu) * lax.rsqrt(var + EPS) * g2_ref[...] + b2_ref[...]


def _final(pos, x1, wts, ys, wsg, wsu, wsd, g2, b2, *, tile, off, n, alpha):
    d = x1.shape[1]
    kern = functools.partial(_final_kernel, alpha=alpha)
    return pl.pallas_call(
        kern,
        grid=(n // tile,),
        in_specs=[pl.BlockSpec((TOP_K, tile), lambda i: (0, i + off), memory_space=pltpu.SMEM),
                  pl.BlockSpec((tile, d), lambda i: (i + off, 0)),
                  pl.BlockSpec((TOP_K, tile), lambda i: (0, i + off)),
                  pl.BlockSpec(memory_space=pl.ANY),
                  _full(wsg.shape), _full(wsu.shape), _full(wsd.shape), _full(g2.shape), _full(b2.shape)],
        out_specs=pl.BlockSpec((tile, d), lambda i: (i, 0)),
        out_shape=jax.ShapeDtypeStruct((n, d), F32),
        scratch_shapes=[pltpu.VMEM((TOP_K, tile * PACK_ROWS, LANES), jnp.uint32), pltpu.SemaphoreType.DMA],
        compiler_params=_cparams(("arbitrary",)),
        name="final",
    )(pos, x1, wts, ys, wsg, wsu, wsd, g2, b2)


def _rope_tables(pos, qk_rope, nh):
    half = qk_rope // 2
    inv = ROPE_BASE ** (-jnp.arange(half, dtype=F32) / half)
    ang = pos.astype(F32)[:, None] * inv
    cos, sin = jnp.cos(ang), jnp.sin(ang)
    c = jnp.concatenate([cos, cos], axis=1)
    s = jnp.concatenate([-sin, sin], axis=1)
    rep = LANES // qk_rope
    rope_k = jnp.concatenate([jnp.tile(c, (1, rep)), jnp.tile(s, (1, rep))], axis=1)
    rope_q = jnp.concatenate([jnp.tile(c, (1, nh)), jnp.tile(s, (1, nh))], axis=1)
    return rope_k, rope_q


def _swap_halves(w, axis):
    a, b = jnp.split(w, 2, axis=axis)
    return jnp.concatenate([b, a], axis=axis)


def _block_diag_pairs(w):
    nh, r, c = w.shape
    z = jnp.zeros((nh // 2, r, c), w.dtype)
    top = jnp.concatenate([w[0::2], z], axis=2)
    bot = jnp.concatenate([z, w[1::2]], axis=2)
    return jnp.concatenate([top, bot], axis=1)


def kernel(x_prompt, x_sample, cache_mla_latent, cache_mla_krope, cache_diff_k, cache_diff_v, w_in, b_gate, g_q_norm, w_uq, w_uk, g_kv_norm, w_uv, w_o_mla, lambda_q1, lambda_k1, lambda_q2, lambda_k2, g_subln, w_o_diff, w_out, g_ln1, b_ln1, w_router, router_bias, w_exp_gate, w_exp_up, w_exp_down, w_sh_gate, w_sh_up, w_sh_down, g_ln2, b_ln2):
    depth = w_in.shape[0]
    assert depth == 1
    b, s, d = x_prompt.shape
    bs, ss, _ = x_sample.shape
    past = cache_mla_latent.shape[2]
    q_lora = g_q_norm.shape[1]
    kv_lora = g_kv_norm.shape[1]
    qk_rope = cache_mla_krope.shape[3]
    nh = w_uq.shape[2]
    qk_nope = w_uq.shape[3] - qk_rope
    mla_v = w_uv.shape[3]
    dnh, _, hd = cache_diff_k.shape[3:]
    diff_w = dnh * 2 * hd
    ne = w_router.shape[2]
    alpha = (2 * depth) ** 0.25
    lam_init = 0.8 - 0.6 * math.exp(-0.3 * 0)
    assert 2 * hd == LANES and kv_lora == LANES and 2 * qk_nope == LANES and LANES % qk_rope == 0

    wi = w_in[0]
    c0 = q_lora + kv_lora
    w_cq, w_ckv = wi[:, :q_lora], wi[:, q_lora:c0]
    w_kr = wi[:, c0:c0 + qk_rope]
    c1 = c0 + qk_rope
    w_d = wi[:, c1:c1 + 3 * diff_w]
    w_gates = wi[:, c1 + 3 * diff_w:]
    rep = LANES // qk_rope
    w1 = jnp.concatenate([w_cq, w_ckv, w_d, jnp.tile(w_kr, (1, rep)),
                          jnp.tile(_swap_halves(w_kr, 1), (1, rep))], axis=1).astype(BF16)
    uq = w_uq[0]
    wqn = uq[:, :, :qk_nope].reshape(q_lora, nh * qk_nope).astype(BF16)
    wqr = uq[:, :, qk_nope:].reshape(q_lora, nh * qk_rope).astype(BF16)
    wqs = _swap_halves(uq[:, :, qk_nope:], 2).reshape(q_lora, nh * qk_rope).astype(BF16)
    wuk = _block_diag_pairs(jnp.transpose(w_uk[0], (1, 2, 0))).astype(BF16)
    wuv = _block_diag_pairs(jnp.transpose(w_uv[0], (1, 0, 2))).astype(BF16)
    lam4 = jnp.concatenate([lambda_q1, lambda_k1, lambda_q2, lambda_k2], axis=0)
    merge_w = (w_gates.astype(BF16), b_gate, w_o_mla[0].astype(BF16), w_o_diff[0].astype(BF16),
               w_out[0].astype(BF16), g_ln1, b_ln1, jnp.transpose(w_router[0]).astype(BF16),
               jnp.transpose(router_bias))
    mla_scale = (qk_nope + qk_rope) ** -0.5
    dims = (q_lora, kv_lora, diff_w, qk_rope, hd ** -0.5 * LOG2E)

    n_p = b * s
    rope_k, rope_q = _rope_tables(jnp.arange(s, dtype=I32), qk_rope, nh)
    cq, lat, kr, kcat, dq, dk, dkb, dv, dvb = _proj(
        x_prompt.reshape(n_p, d), w1, g_q_norm, g_kv_norm, rope_k, tile=PROJ_TILE, dims=dims)
    r3 = lambda a: a.reshape(b, s, a.shape[1])
    oa = _mla(r3(cq), rope_q, r3(kcat), wqn, wqr, wqs, wuk, wuv, qb=ATT_BLOCK, kb=ATT_BLOCK, nh=nh,
              qk_rope=qk_rope, past=0, n_valid=s, scale=mla_scale, causal=True)
    ob = _diff(r3(dq), r3(dkb), r3(dvb), lam4, g_subln, qb=ATT_BLOCK, kb=ATT_BLOCK, nh=dnh, hd=hd,
               past=0, n_valid=s, causal=True, lam_init=lam_init)

    n_s = bs * ss
    sk = past + ss
    skp = -(-sk // LANES) * LANES
    rope_ks, rope_qs = _rope_tables(past + jnp.arange(ss, dtype=I32), qk_rope, nh)
    cq_s, lat_s, kr_s, kcat_s, dq_s, dk_s, dkb_s, dv_s, dvb_s = _proj(
        x_sample.reshape(n_s, d), w1, g_q_norm, g_kv_norm, jnp.tile(rope_ks, (bs, 1)), tile=n_s, dims=dims)
    r3s = lambda a: a.reshape(bs, ss, a.shape[1])
    padk = lambda a: jnp.pad(a, ((0, 0), (0, skp - sk), (0, 0)))
    kcat_all = padk(jnp.concatenate(
        [jnp.concatenate([cache_mla_latent[0], jnp.tile(cache_mla_krope[0], (1, 1, rep))], axis=2).astype(BF16),
         r3s(kcat_s)], axis=1))
    dk_all = padk(jnp.concatenate([cache_diff_k[0].reshape(bs, past, diff_w).astype(BF16), r3s(dkb_s)], axis=1))
    dv_all = padk(jnp.concatenate([cache_diff_v[0].reshape(bs, past, diff_w).astype(BF16), r3s(dvb_s)], axis=1))
    oa_s = _mla(r3s(cq_s), rope_qs, kcat_all, wqn, wqr, wqs, wuk, wuv, qb=ss, kb=skp, nh=nh,
                qk_rope=qk_rope, past=past, n_valid=sk, scale=mla_scale, causal=False)
    ob_s = _diff(r3s(dq_s), dk_all, dv_all, lam4, g_subln, qb=ss, kb=skp, nh=dnh, hd=hd,
                 past=past, n_valid=sk, causal=False, lam_init=lam_init)

    ntot = n_p + n_s
    x1, x1p, idx, wts = _merge(x_prompt.reshape(n_p, d), x_sample.reshape(n_s, d), oa.reshape(n_p, -1),
                          oa_s.reshape(n_s, -1), ob.reshape(n_p, -1), ob_s.reshape(n_s, -1), merge_w,
                          tile=TOK_TILE, alpha=alpha, ne=ne)
    nblocks = -(-(ntot * TOP_K) // EXP_BLOCK) + ne
    nbp = -(-nblocks // LANES) * LANES
    pos, be, bnv, nused = _plan(idx, ne=ne, blk=EXP_BLOCK, nbp=nbp, tile=TOK_TILE)
    xs = _scatter(pos, x1p, rows=nblocks * EXP_BLOCK, tile=TOK_TILE)
    ys = _ffn(be[0, :nblocks], bnv[0, :nblocks], nused[0, :1], xs, w_exp_gate[0], w_exp_up[0], w_exp_down[0],
              blk=EXP_BLOCK)
    fin_w = (w_sh_gate[0].astype(BF16), w_sh_up[0].astype(BF16), w_sh_down[0].astype(BF16), g_ln2, b_ln2)
    y_p = _final(pos, x1, wts, ys, *fin_w, tile=TOK_TILE, off=0, n=n_p, alpha=alpha)
    y_s = _final(pos, x1, wts, ys, *fin_w, tile=TOK_TILE, off=n_p // TOK_TILE, n=n_s, alpha=alpha)

    st = lambda a, bb, sq, tail: a.reshape((1, bb, sq) + tail)
    return (y_p.reshape(b, s, d), y_s.reshape(bs, ss, d),
            st(lat, b, s, (kv_lora,)), st(kr, b, s, (qk_rope,)),
            st(dk, b, s, (dnh, 2, hd)), st(dv, b, s, (dnh, 2 * hd)),
            st(lat_s, bs, ss, (kv_lora,)), st(kr_s, bs, ss, (qk_rope,)),
            st(dk_s, bs, ss, (dnh, 2, hd)), st(dv_s, bs, ss, (dnh, 2 * hd)))
```

```python
import functools
import math

import jax
import jax.numpy as jnp
from jax import lax
from jax.experimental import pallas as pl
from jax.experimental.pallas import tpu as pltpu

F32 = jnp.float32
BF16 = jnp.bfloat16
I32 = jnp.int32

CHUNK = 64
ROPE_BASE = 10000.0
EPS = 1e-6
N_GROUPS = 8
TOPK_GROUPS = 4
TOP_K = 8
ROUTED_SCALE = 2.5
NEG = -1e30

LANES = 128
TOK_TILE = 256
PROJ_TILE = 512
ATT_BLOCK = 512
SLAB_ROWS = 256
EXP_BLOCK = 512
VMEM_LIMIT = 56 * 1024 * 1024


def _cparams(sem):
    return pltpu.CompilerParams(dimension_semantics=sem, vmem_limit_bytes=VMEM_LIMIT)


def _full(shape):
    nd = len(shape)
    return pl.BlockSpec(shape, lambda *_: (0,) * nd)


def _proj_kernel(x_ref, w_ref, gq_ref, gkv_ref, rope_ref,
                 cq_ref, lat_ref, kr_ref, kcat_ref, dq_ref, dk_ref, dkb_ref, dv_ref, dvb_ref,
                 *, q_lora, kv_lora, diff_w, qk_rope, dq_scale):
    x = x_ref[...].astype(BF16)
    h = jnp.dot(x, w_ref[...], preferred_element_type=F32)
    o = 0
    cq = h[:, o:o + q_lora]
    o += q_lora
    cq = cq * lax.rsqrt(jnp.mean(cq * cq, axis=-1, keepdims=True) + EPS) * gq_ref[...]
    cq_ref[...] = cq.astype(BF16)
    ckv = h[:, o:o + kv_lora]
    o += kv_lora
    lat = ckv * lax.rsqrt(jnp.mean(ckv * ckv, axis=-1, keepdims=True) + EPS) * gkv_ref[...]
    lat_ref[...] = lat
    dq_ref[...] = (h[:, o:o + diff_w] * dq_scale).astype(BF16)
    o += diff_w
    dk = h[:, o:o + diff_w]
    o += diff_w
    dk_ref[...] = dk
    dkb_ref[...] = dk.astype(BF16)
    dv = h[:, o:o + diff_w]
    o += diff_w
    dv_ref[...] = dv
    dvb_ref[...] = dv.astype(BF16)
    krr = h[:, o:o + LANES] * rope_ref[:, 0:LANES] + h[:, o + LANES:o + 2 * LANES] * rope_ref[:, LANES:2 * LANES]
    kr_ref[...] = krr[:, 0:qk_rope]
    kcat_ref[...] = jnp.concatenate([lat, krr], axis=1).astype(BF16)


def _proj(x, w1, gq, gkv, rope_k, *, tile, dims):
    n, d = x.shape
    q_lora, kv_lora, diff_w, qk_rope, dq_scale = dims
    wcols = w1.shape[1]
    nrope = rope_k.shape[0] // tile
    row = lambda c: pl.BlockSpec((tile, c), lambda i: (i, 0))
    outs = [
        jax.ShapeDtypeStruct((n, q_lora), BF16),
        jax.ShapeDtypeStruct((n, kv_lora), F32),
        jax.ShapeDtypeStruct((n, qk_rope), F32),
        jax.ShapeDtypeStruct((n, 2 * LANES), BF16),
        jax.ShapeDtypeStruct((n, diff_w), BF16),
        jax.ShapeDtypeStruct((n, diff_w), F32),
        jax.ShapeDtypeStruct((n, diff_w), BF16),
        jax.ShapeDtypeStruct((n, diff_w), F32),
        jax.ShapeDtypeStruct((n, diff_w), BF16),
    ]
    return pl.pallas_call(
        functools.partial(_proj_kernel, q_lora=q_lora, kv_lora=kv_lora, diff_w=diff_w, qk_rope=qk_rope,
                          dq_scale=dq_scale),
        grid=(n // tile,),
        in_specs=[row(d), _full((d, wcols)), _full((1, q_lora)), _full((1, kv_lora)),
                  pl.BlockSpec((tile, 2 * LANES), lambda i: (i % nrope, 0))],
        out_specs=[row(q_lora), row(kv_lora), row(qk_rope), row(2 * LANES), row(diff_w), row(diff_w),
                   row(diff_w), row(diff_w), row(diff_w)],
        out_shape=outs,
        compiler_params=_cparams(("arbitrary",)),
        name="proj",
    )(x, w1, gq, gkv, rope_k)


LOG2E = math.log2(math.e)


def _softmax_step(s2, v, m_scr, accl_scr, r0, rows, kb):
    m_prev = m_scr[r0:r0 + rows, :]
    m_cur = jnp.max(s2, axis=1, keepdims=True).astype(BF16).astype(F32)
    m_next = jnp.maximum(m_prev, m_cur)
    mb = m_next.astype(BF16)
    p = jnp.exp2(s2.astype(BF16) - jnp.concatenate([mb] * (kb // LANES), axis=1))
    alpha = jnp.exp2(m_prev - m_next)
    m_scr[r0:r0 + rows, :] = m_next
    v1 = jnp.concatenate([v, jnp.ones(v.shape, BF16)], axis=1)
    pv = jnp.dot(p, v1, preferred_element_type=F32)
    accl_scr[r0:r0 + rows, :] = accl_scr[r0:r0 + rows, :] * jnp.concatenate([alpha, alpha], axis=1) + pv


def _visible(qb, kb, q0, k0, n_valid):
    qpos = q0 + lax.broadcasted_iota(I32, (qb, kb), 0)
    kpos = k0 + lax.broadcasted_iota(I32, (qb, kb), 1)
    sh = CHUNK.bit_length() - 1
    vis = (lax.shift_right_logical(kpos, sh) <= lax.shift_right_logical(qpos, sh)) & (kpos < n_valid)
    return vis, qpos, kpos


def _slab_rows(x, r0, rp, qb):
    if rp >= qb:
        return x[None]
    q0 = r0 % qb
    return x[q0:q0 + rp]


def _mask_slab(s2, vis, r0, rp, qb, w):
    if rp >= qb:
        return jnp.where(vis[None], s2.reshape(rp // qb, qb, w), NEG).reshape(rp, w)
    return jnp.where(_slab_rows(vis, r0, rp, qb), s2, NEG)


def _key_blocks(i, kb, causal, block):
    if not causal:
        block(0, kb, True)
        return

    def body(j, c):
        block(pl.multiple_of(j * kb, kb), kb, False)
        return c

    lax.fori_loop(0, i, body, 0)
    block(pl.multiple_of(i * kb, kb), kb, True)


def _mla_kernel(cq_ref, ropeq_ref, kcat_ref, wqn_ref, wqr_ref, wqs_ref, wuk_ref, wuv_ref, o_ref,
                qcat_scr, m_scr, accl_scr, *, qb, kb, nh, qk_rope, past, n_valid, scale, causal):
    i = pl.program_id(1)
    rows = nh * qb
    c2 = scale * LOG2E
    cq = cq_ref[0]
    qn = jnp.dot(cq, wqn_ref[...], preferred_element_type=F32).astype(BF16)
    qr = jnp.dot(cq, wqr_ref[...], preferred_element_type=F32)
    qs = jnp.dot(cq, wqs_ref[...], preferred_element_type=F32)
    rw = nh * qk_rope
    rot = (qr * ropeq_ref[:, 0:rw] + qs * ropeq_ref[:, rw:2 * rw]) * c2
    lane = lax.broadcasted_iota(I32, (qb, LANES), 1)
    per = LANES // qk_rope
    for j in range(nh // 2):
        ql = jnp.dot(qn[:, LANES * j:LANES * (j + 1)], wuk_ref[j], preferred_element_type=F32) * c2
        for u in range(2):
            hh = 2 * j + u
            rblk = rot[:, LANES * (hh // per):LANES * (hh // per + 1)]
            lo = qk_rope * (hh % per)
            part2 = jnp.where((lane >= lo) & (lane < lo + qk_rope), rblk, 0.0)
            qcat_scr[hh * qb:(hh + 1) * qb, :] = jnp.concatenate(
                [ql[:, LANES * u:LANES * (u + 1)], part2], axis=1).astype(BF16)
    m_scr[...] = jnp.full(m_scr.shape, NEG, F32)
    accl_scr[...] = jnp.zeros(accl_scr.shape, F32)

    def block(k0, w, masked):
        kblk = kcat_ref[0, pl.ds(k0, w), :]
        if masked:
            vis, _, _ = _visible(qb, w, past + i * qb, k0, n_valid)
        rp = min(rows, SLAB_ROWS)
        for sl in range(rows // rp):
            r0 = sl * rp
            s2 = lax.dot_general(qcat_scr[r0:r0 + rp, :], kblk, (((1,), (1,)), ((), ())),
                                 preferred_element_type=F32)
            if masked:
                s2 = _mask_slab(s2, vis, r0, rp, qb, w)
            _softmax_step(s2, kblk[:, 0:LANES], m_scr, accl_scr, r0, rp, w)

    _key_blocks(i, kb, causal, block)

    o_lat = accl_scr[:, 0:LANES] / accl_scr[:, LANES:2 * LANES]
    for j in range(nh // 2):
        pair = jnp.concatenate([o_lat[(2 * j) * qb:(2 * j + 1) * qb, :],
                                o_lat[(2 * j + 1) * qb:(2 * j + 2) * qb, :]], axis=1).astype(BF16)
        o_ref[0, :, LANES * j:LANES * (j + 1)] = jnp.dot(
            pair, wuv_ref[j], preferred_element_type=F32).astype(BF16)


def _mla(cq, rope_q, kcat, wqn, wqr, wqs, wuk, wuv, *, qb, kb, nh, qk_rope, past, n_valid, scale, causal):
    b, sq, q_lora = cq.shape
    sk = kcat.shape[1]
    nq = sq // qb
    rows = nh * qb
    ov = wuv.shape[0] * wuv.shape[2]
    kern = functools.partial(_mla_kernel, qb=qb, kb=kb, nh=nh, qk_rope=qk_rope, past=past,
                             n_valid=n_valid, scale=scale, causal=causal)
    return pl.pallas_call(
        kern,
        grid=(b, nq),
        in_specs=[pl.BlockSpec((1, qb, q_lora), lambda bb, i: (bb, i, 0)),
                  pl.BlockSpec((qb, rope_q.shape[1]), lambda bb, i: (i, 0)),
                  pl.BlockSpec((1, sk, kcat.shape[2]), lambda bb, i: (bb, 0, 0)),
                  _full(wqn.shape), _full(wqr.shape), _full(wqs.shape), _full(wuk.shape), _full(wuv.shape)],
        out_specs=pl.BlockSpec((1, qb, ov), lambda bb, i: (bb, i, 0)),
        out_shape=jax.ShapeDtypeStruct((b, sq, ov), BF16),
        scratch_shapes=[pltpu.VMEM((rows, 2 * LANES), BF16), pltpu.VMEM((rows, LANES), F32),
                        pltpu.VMEM((rows, 2 * LANES), F32)],
        compiler_params=_cparams(("arbitrary", "arbitrary")),
        name="mla",
    )(cq, rope_q, kcat, wqn, wqr, wqs, wuk, wuv)


def _diff_kernel(dq_ref, dk_ref, dv_ref, lam_ref, gsub_ref, o_ref, qq_scr, m_scr, accl_scr,
                 *, qb, kb, nh, hd, past, n_valid, causal, lam_init):
    i = pl.program_id(1)
    q = dq_ref[0]
    lane = lax.broadcasted_iota(I32, (qb, LANES), 1)
    zero = jnp.zeros((), BF16)
    for h in range(nh):
        qh = q[:, LANES * h:LANES * (h + 1)]
        qq_scr[(2 * h) * qb:(2 * h + 1) * qb, :] = jnp.where(lane < hd, qh, zero)
        qq_scr[(2 * h + 1) * qb:(2 * h + 2) * qb, :] = jnp.where(lane >= hd, qh, zero)
    m_scr[...] = jnp.full(m_scr.shape, NEG, F32)
    accl_scr[...] = jnp.zeros(accl_scr.shape, F32)

    def block(k0, w, masked):
        vis, qpos, kpos = _visible(qb, w, past + i * qb, k0, n_valid)
        dist = jnp.abs(qpos - kpos).astype(F32)
        for h in range(nh):
            slope2 = 2.0 ** (-8.0 * (h + 1) / nh) * LOG2E
            kh = dk_ref[0, pl.ds(k0, w), LANES * h:LANES * (h + 1)]
            vh = dv_ref[0, pl.ds(k0, w), LANES * h:LANES * (h + 1)]
            bias = slope2 * dist
            rp = min(2 * qb, SLAB_ROWS)
            for sl in range(2 * qb // rp):
                r0 = 2 * h * qb + sl * rp
                s2 = lax.dot_general(qq_scr[r0:r0 + rp, :], kh, (((1,), (1,)), ((), ())),
                                     preferred_element_type=F32)
                if rp >= qb:
                    s2 = (s2.reshape(rp // qb, qb, w) - bias[None]).reshape(rp, w)
                else:
                    s2 = s2 - _slab_rows(bias, r0, rp, qb)
                if masked:
                    s2 = _mask_slab(s2, vis, r0, rp, qb, w)
                _softmax_step(s2, vh, m_scr, accl_scr, r0, rp, w)

    _key_blocks(i, kb, causal, block)

    lam1 = jnp.sum(lam_ref[0:1, :] * lam_ref[1:2, :], axis=1, keepdims=True)
    lam2 = jnp.sum(lam_ref[2:3, :] * lam_ref[3:4, :], axis=1, keepdims=True)
    lam = jnp.exp(lam1) - jnp.exp(lam2) + lam_init
    o_all = accl_scr[:, 0:LANES] / accl_scr[:, LANES:2 * LANES]
    for h in range(nh):
        o = o_all[(2 * h) * qb:(2 * h + 1) * qb, :] - lam * o_all[(2 * h + 1) * qb:(2 * h + 2) * qb, :]
        o = o * lax.rsqrt(jnp.mean(o * o, axis=-1, keepdims=True) + EPS) * gsub_ref[...]
        o_ref[0, :, LANES * h:LANES * (h + 1)] = (o * (1.0 - lam_init)).astype(BF16)


def _diff(dq, dk, dv, lam4, gsub, *, qb, kb, nh, hd, past, n_valid, causal, lam_init):
    b, sq, w = dq.shape
    sk = dk.shape[1]
    nq = sq // qb
    rows = 2 * nh * qb
    kern = functools.partial(_diff_kernel, qb=qb, kb=kb, nh=nh, hd=hd, past=past, n_valid=n_valid,
                             causal=causal, lam_init=lam_init)
    return pl.pallas_call(
        kern,
        grid=(b, nq),
        in_specs=[pl.BlockSpec((1, qb, w), lambda bb, i: (bb, i, 0)),
                  pl.BlockSpec((1, sk, w), lambda bb, i: (bb, 0, 0)),
                  pl.BlockSpec((1, sk, w), lambda bb, i: (bb, 0, 0)),
                  _full(lam4.shape), _full(gsub.shape)],
        out_specs=pl.BlockSpec((1, qb, w), lambda bb, i: (bb, i, 0)),
        out_shape=jax.ShapeDtypeStruct((b, sq, w), BF16),
        scratch_shapes=[pltpu.VMEM((rows, LANES), BF16), pltpu.VMEM((rows, LANES), F32),
                        pltpu.VMEM((rows, 2 * LANES), F32)],
        compiler_params=_cparams(("arbitrary", "arbitrary")),
        name="diffattn",
    )(dq, dk, dv, lam4, gsub)


def _red2(x, fn):
    return fn(fn(x, axis=0, keepdims=True), axis=1, keepdims=True)


PACK_ROWS = 4


def _store_packed(ref, x):
    rows, d = x.shape
    bits = pltpu.bitcast(x.astype(BF16).astype(F32), jnp.uint32)
    packed = bits[:, 0:d // 2] | (bits[:, d // 2:d] >> 16)
    for j in range(PACK_ROWS):
        ref[pl.ds(j, rows, stride=PACK_ROWS), :] = packed[:, LANES * j:LANES * (j + 1)]


def _load_packed(ref, rows):
    words = [ref[pl.ds(j, rows, stride=PACK_ROWS), :] for j in range(PACK_ROWS)]
    hi = [pltpu.bitcast(w & jnp.uint32(0xFFFF0000), F32) for w in words]
    lo = [pltpu.bitcast(w << 16, F32) for w in words]
    return jnp.concatenate(hi + lo, axis=1)


def _merge_kernel(xp_ref, xs_ref, oap_ref, oas_ref, obp_ref, obs_ref, wg_ref, bg_ref, woa_ref, wob_ref,
                  wout_ref, g1_ref, b1_ref, wrt_ref, rb_ref, x1_ref, x1p_ref, idx_ref, wts_ref, *, alpha, d, ne, ntp):
    is_p = pl.program_id(0) < ntp
    x = jnp.where(is_p, xp_ref[...], xs_ref[...])
    oa = jnp.where(is_p, oap_ref[...], oas_ref[...])
    ob = jnp.where(is_p, obp_ref[...], obs_ref[...])
    gates = jnp.dot(x.astype(BF16), wg_ref[...], preferred_element_type=F32) + bg_ref[...]
    g = jax.nn.sigmoid(gates)
    br_a = jnp.dot(oa, woa_ref[...], preferred_element_type=F32)
    br_b = jnp.dot(ob, wob_ref[...], preferred_element_type=F32)
    mixin = g[:, 0:d] * br_a + g[:, d:2 * d] * br_b
    mix = jnp.dot(mixin.astype(BF16), wout_ref[...], preferred_element_type=F32)
    z = alpha * x + mix
    mu = jnp.mean(z, axis=-1, keepdims=True)
    var = jnp.mean(jnp.square(z - mu), axis=-1, keepdims=True)
    x1 = (z - mu) * lax.rsqrt(var + EPS) * g1_ref[...] + b1_ref[...]
    x1_ref[...] = x1
    _store_packed(x1p_ref, x1)

    tm = x.shape[0]
    gsz = ne // N_GROUPS
    logits = lax.dot_general(wrt_ref[...], x1.astype(BF16), (((1,), (1,)), ((), ())),
                             preferred_element_type=F32)
    scores = jax.nn.sigmoid(logits)
    choice = scores + rb_ref[...]
    c3 = choice.reshape(N_GROUPS, gsz, tm)
    s3 = scores.reshape(N_GROUPS, gsz, tm)
    io = lax.broadcasted_iota(I32, (N_GROUPS, gsz, tm), 1)
    gio = lax.broadcasted_iota(I32, (N_GROUPS, gsz, tm), 0)
    eio = gio * gsz + io
    ninf = -jnp.inf
    m1 = jnp.max(c3, axis=1, keepdims=True)
    i1 = jnp.min(jnp.where(c3 == m1, io, gsz), axis=1, keepdims=True)
    m2 = jnp.max(jnp.where(io == i1, ninf, c3), axis=1, keepdims=True)
    gs = m1 + m2
    gio1 = lax.broadcasted_iota(I32, (N_GROUPS, 1, tm), 0)
    gsel = jnp.zeros((N_GROUPS, 1, tm), F32)
    cur = gs
    for _ in range(TOPK_GROUPS):
        gm = jnp.max(cur, axis=0, keepdims=True)
        gi = jnp.min(jnp.where(cur == gm, gio1, N_GROUPS), axis=0, keepdims=True)
        hit = gio1 == gi
        gsel = jnp.where(hit, 1.0, gsel)
        cur = jnp.where(hit, ninf, cur)
    cur = jnp.where(gsel > 0.5, c3, ninf)
    idx_rows, w_rows = [], []
    for _ in range(TOP_K):
        m = _red2(cur, jnp.max)
        ik = _red2(jnp.where(cur == m, eio, ne), jnp.min)
        hit = eio == ik
        w_rows.append(_red2(jnp.where(hit, s3, 0.0), jnp.sum)[0])
        idx_rows.append(ik[0])
        cur = jnp.where(hit, ninf, cur)
    wsum = w_rows[0]
    for w in w_rows[1:]:
        wsum = wsum + w
    for k in range(TOP_K):
        idx_ref[k:k + 1, :] = idx_rows[k]
        wts_ref[k:k + 1, :] = w_rows[k] / wsum * ROUTED_SCALE


def _merge(xp, xs, oap, oas, obp, obs, weights, *, tile, alpha, ne):
    n_p, d = xp.shape
    n_s = xs.shape[0]
    ntp, nts = n_p // tile, n_s // tile
    ntot = n_p + n_s
    prow = lambda c: pl.BlockSpec((tile, c), lambda i: (jnp.minimum(i, ntp - 1), 0))
    srow = lambda c: pl.BlockSpec((tile, c), lambda i: (jnp.maximum(i - ntp, 0), 0))
    wa, wb = oap.shape[1], obp.shape[1]
    return pl.pallas_call(
        functools.partial(_merge_kernel, alpha=alpha, d=d, ne=ne, ntp=ntp),
        grid=(ntp + nts,),
        in_specs=[prow(d), srow(d), prow(wa), srow(wa), prow(wb), srow(wb)] + [_full(a.shape) for a in weights],
        out_specs=[pl.BlockSpec((tile, d), lambda i: (i, 0)),
                   pl.BlockSpec((tile * PACK_ROWS, LANES), lambda i: (i, 0)),
                   pl.BlockSpec((TOP_K, tile), lambda i: (0, i)),
                   pl.BlockSpec((TOP_K, tile), lambda i: (0, i))],
        out_shape=[jax.ShapeDtypeStruct((ntot, d), F32),
                   jax.ShapeDtypeStruct((ntot * PACK_ROWS, LANES), jnp.uint32),
                   jax.ShapeDtypeStruct((TOP_K, ntot), I32),
                   jax.ShapeDtypeStruct((TOP_K, ntot), F32)],
        compiler_params=_cparams(("arbitrary",)),
        name="merge",
    )(xp, xs, oap, oas, obp, obs, *weights)


def _plan_kernel(idx_ref, ltri_ref, su_ref, pos_ref, be_ref, bnv_ref, bnext_ref, nused_ref, cnt_scr, base_scr,
                 *, ne, blk, nbp):
    ph = pl.program_id(0)
    i = pl.program_id(1)
    tm = idx_ref.shape[1]
    eio = lax.broadcasted_iota(I32, (ne, tm), 0)
    hits = [eio == idx_ref[k:k + 1, :] for k in range(TOP_K)]
    mh = jnp.zeros((ne, tm), F32)
    for hk in hits:
        mh = jnp.where(hk, 1.0, mh)
    rowsum = jnp.sum(mh, axis=1, keepdims=True)

    @pl.when((ph == 0) & (i == 0))
    def _():
        cnt_scr[...] = jnp.zeros(cnt_scr.shape, F32)

    @pl.when(ph == 0)
    def _():
        cnt_scr[...] = cnt_scr[...] + rowsum

    @pl.when((ph == 1) & (i == 0))
    def _():
        cnt = cnt_scr[...]
        sh = blk.bit_length() - 1
        nb = lax.shift_right_logical(cnt.astype(I32) + (blk - 1), sh)
        nbf = nb.astype(F32)
        start = jnp.dot(ltri_ref[...], jnp.broadcast_to(nbf, (ne, LANES)).astype(BF16),
                        preferred_element_type=F32)[:, 0:1]
        end = start + nbf
        base_scr[...] = start * float(blk)
        bio = lax.broadcasted_iota(I32, (ne, nbp), 1).astype(F32)
        be = jnp.sum(jnp.where(end <= bio, 1.0, 0.0), axis=0, keepdims=True)
        be_ref[...] = jnp.minimum(be, float(ne - 1)).astype(I32)
        inside = (start <= bio) & (bio < end)
        left = jnp.clip(cnt - (bio - start) * float(blk), 0.0, float(blk))
        bnv_ref[...] = jnp.sum(jnp.where(inside, left, 0.0), axis=0, keepdims=True).astype(I32)
        bnext_ref[...] = jnp.sum(jnp.where(inside, end, 0.0), axis=0, keepdims=True).astype(I32)
        nused_ref[...] = jnp.broadcast_to(jnp.sum(nbf, axis=0, keepdims=True), (1, LANES)).astype(I32)

    @pl.when(ph == 1)
    def _():
        before = jnp.dot(mh.astype(BF16), su_ref[...], preferred_element_type=F32)
        rank = base_scr[...] + before
        for k in range(TOP_K):
            row = jnp.sum(jnp.where(hits[k], rank, 0.0), axis=0, keepdims=True).astype(I32)
            pos_ref[k:k + 1, :] = row * PACK_ROWS
        base_scr[...] = base_scr[...] + rowsum


def _plan(idx, *, ne, blk, nbp, tile):
    ntot = idx.shape[1]
    nt = ntot // tile
    ltri = jnp.tril(jnp.ones((ne, ne), F32), -1).astype(BF16)
    su = jnp.triu(jnp.ones((tile, tile), F32), 1).astype(BF16)
    kern = functools.partial(_plan_kernel, ne=ne, blk=blk, nbp=nbp)
    meta = lambda: pl.BlockSpec((1, nbp), lambda p, i: (0, 0))
    return pl.pallas_call(
        kern,
        grid=(2, nt),
        in_specs=[pl.BlockSpec((TOP_K, tile), lambda p, i: (0, i)), _full((ne, ne)), _full((tile, tile))],
        out_specs=[pl.BlockSpec((TOP_K, tile), lambda p, i: (0, i * p)), meta(), meta(), meta(),
                   pl.BlockSpec((1, LANES), lambda p, i: (0, 0))],
        out_shape=[jax.ShapeDtypeStruct((TOP_K, ntot), I32), jax.ShapeDtypeStruct((1, nbp), I32),
                   jax.ShapeDtypeStruct((1, nbp), I32), jax.ShapeDtypeStruct((1, nbp), I32),
                   jax.ShapeDtypeStruct((1, LANES), I32)],
        scratch_shapes=[pltpu.VMEM((ne, 1), F32), pltpu.VMEM((ne, 1), F32)],
        compiler_params=_cparams(("arbitrary", "arbitrary")),
        name="plan",
    )(idx, ltri, su)


def _scatter_kernel(pos_ref, x_ref, xs_ref, sem):
    tm = pos_ref.shape[1]

    for t in range(tm):
        src = x_ref.at[pl.ds(t * PACK_ROWS, PACK_ROWS), :]
        for k in range(TOP_K):
            p = pl.multiple_of(pos_ref[k, t], PACK_ROWS)
            pltpu.make_async_copy(src, xs_ref.at[pl.ds(p, PACK_ROWS), :], sem).start(priority=k % 2)
    for _ in range(TOP_K):
        pltpu.make_async_copy(x_ref, xs_ref.at[pl.ds(0, tm * PACK_ROWS), :], sem).wait()


def _scatter(pos, x1p, *, rows, tile):
    ntot = pos.shape[1]
    return pl.pallas_call(
        _scatter_kernel,
        grid=(ntot // tile,),
        in_specs=[pl.BlockSpec((TOP_K, tile), lambda i: (0, i), memory_space=pltpu.SMEM),
                  pl.BlockSpec((tile * PACK_ROWS, LANES), lambda i: (i, 0))],
        out_specs=pl.BlockSpec(memory_space=pl.ANY),
        out_shape=jax.ShapeDtypeStruct((rows * PACK_ROWS, LANES), jnp.uint32),
        scratch_shapes=[pltpu.SemaphoreType.DMA],
        compiler_params=_cparams(("arbitrary",)),
        name="scatter",
    )(pos, x1p)


def _ffn_kernel(be_ref, bnv_ref, bnext_ref, nused_ref, xs_ref, wg_ref, wu_ref, wd_ref, ys_ref,
                xbuf, ybuf, wgf, wuf, wdf, wgb, wub, wdb, xsem, ysem, wsem, *, blk):
    nused = nused_ref[0]
    prow = blk * PACK_ROWS

    def x_copy(g, slot):
        return pltpu.make_async_copy(xs_ref.at[pl.ds(pl.multiple_of(g * prow, prow), prow), :],
                                     xbuf.at[slot], xsem.at[slot])

    def y_copy(g, slot):
        return pltpu.make_async_copy(ybuf.at[slot], ys_ref.at[pl.ds(pl.multiple_of(g * prow, prow), prow), :],
                                     ysem.at[slot])

    def w_copies(e, slot):
        return (pltpu.make_async_copy(wg_ref.at[e], wgf.at[slot], wsem.at[slot]),
                pltpu.make_async_copy(wu_ref.at[e], wuf.at[slot], wsem.at[slot]),
                pltpu.make_async_copy(wd_ref.at[e], wdf.at[slot], wsem.at[slot]))

    @pl.when(nused > 0)
    def _():
        x_copy(0, 0).start()
        for c in w_copies(be_ref[0], 0):
            c.start()

    def body(g, ws):
        e = be_ref[g]
        first = (g == 0) | (be_ref[jnp.maximum(g - 1, 0)] != e)
        ws = jnp.where(first & (g > 0), 1 - ws, ws)

        @pl.when(first)
        def _():
            for c in w_copies(e, ws):
                c.wait()
            wgb[...] = wgf[ws].astype(BF16)
            wub[...] = wuf[ws].astype(BF16)
            wdb[...] = wdf[ws].astype(BF16)
            gn = bnext_ref[g]

            @pl.when(gn < nused)
            def _():
                for c in w_copies(be_ref[jnp.minimum(gn, nused - 1)], 1 - ws):
                    c.start()

        slot = g & 1

        @pl.when(g + 1 < nused)
        def _():
            x_copy(g + 1, 1 - slot).start()

        x_copy(g, slot).wait()
        rid = lax.broadcasted_iota(I32, (blk, 1), 0)
        x = jnp.where(rid < bnv_ref[g], _load_packed(xbuf.at[slot], blk), 0.0).astype(BF16)
        gg = jnp.dot(x, wgb[...], preferred_element_type=F32)
        uu = jnp.dot(x, wub[...], preferred_element_type=F32)
        h = (gg * jax.nn.sigmoid(gg)) * uu
        y = jnp.dot(h.astype(BF16), wdb[...], preferred_element_type=F32)

        @pl.when(g >= 2)
        def _():
            y_copy(g - 2, slot).wait()

        _store_packed(ybuf.at[slot], y)
        y_copy(g, slot).start()
        return ws

    lax.fori_loop(0, nused, body, jnp.int32(0))

    @pl.when(nused >= 2)
    def _():
        y_copy(nused - 2, nused & 1).wait()

    @pl.when(nused >= 1)
    def _():
        y_copy(nused - 1, (nused - 1) & 1).wait()


def _ffn(be, bnv, bnext, nused, xs, wg, wu, wd, *, blk):
    d, de = wg.shape[1], wg.shape[2]
    prow = blk * PACK_ROWS
    any_spec = pl.BlockSpec(memory_space=pl.ANY)
    grid_spec = pltpu.PrefetchScalarGridSpec(
        num_scalar_prefetch=4,
        grid=(1,),
        in_specs=[any_spec, any_spec, any_spec, any_spec],
        out_specs=any_spec,
        scratch_shapes=[pltpu.VMEM((2, prow, LANES), jnp.uint32), pltpu.VMEM((2, prow, LANES), jnp.uint32),
                        pltpu.VMEM((2, d, de), F32), pltpu.VMEM((2, d, de), F32), pltpu.VMEM((2, de, d), F32),
                        pltpu.VMEM((d, de), BF16), pltpu.VMEM((d, de), BF16), pltpu.VMEM((de, d), BF16),
                        pltpu.SemaphoreType.DMA((2,)), pltpu.SemaphoreType.DMA((2,)),
                        pltpu.SemaphoreType.DMA((2,))],
    )
    return pl.pallas_call(
        functools.partial(_ffn_kernel, blk=blk),
        grid_spec=grid_spec,
        out_shape=jax.ShapeDtypeStruct(xs.shape, jnp.uint32),
        compiler_params=_cparams(("arbitrary",)),
        name="ffn",
    )(be, bnv, bnext, nused, xs, wg, wu, wd)


def _final_kernel(pos_ref, posn_ref, x1_ref, wts_ref, ys_ref, wsg_ref, wsu_ref, wsd_ref, g2_ref, b2_ref, y_ref,
                  gbuf, sem, *, alpha):
    i = pl.program_id(0)
    n = pl.num_programs(0)
    tm = x1_ref.shape[0]

    def gather(p_ref, slot, t, dst):
        for k in range(TOP_K):
            p = pl.multiple_of(p_ref[k, t], PACK_ROWS)
            pltpu.make_async_copy(ys_ref.at[pl.ds(p, PACK_ROWS), :], gbuf.at[slot, k, pl.ds(dst, PACK_ROWS), :],
                                  sem.at[slot]).start(priority=k % 2)

    def issue(p_ref, slot, unrolled):
        if unrolled:
            for t in range(tm):
                gather(p_ref, slot, t, t * PACK_ROWS)
        else:
            def tok(t, c):
                gather(p_ref, slot, t, pl.multiple_of(t * PACK_ROWS, PACK_ROWS))
                return c
            lax.fori_loop(0, tm, tok, 0)

    def wait(slot):
        for k in range(TOP_K):
            pltpu.make_async_copy(ys_ref.at[pl.ds(0, tm * PACK_ROWS), :], gbuf.at[slot, k], sem.at[slot]).wait()

    def combine(slot):
        x1 = x1_ref[...]
        xb = x1.astype(BF16)
        g = jnp.dot(xb, wsg_ref[...], preferred_element_type=F32)
        u = jnp.dot(xb, wsu_ref[...], preferred_element_type=F32)
        shared = jnp.dot(((g * jax.nn.sigmoid(g)) * u).astype(BF16), wsd_ref[...], preferred_element_type=F32)
        eye = lax.broadcasted_iota(I32, (tm, tm), 0) == lax.broadcasted_iota(I32, (tm, tm), 1)
        routed = jnp.zeros(x1.shape, F32)
        for k in range(TOP_K):
            wcol = jnp.sum(jnp.where(eye, wts_ref[k:k + 1, :], 0.0), axis=1, keepdims=True)
            routed = routed + _load_packed(gbuf.at[slot, k], tm) * wcol
        z = alpha * x1 + (routed + shared)
        mu = jnp.mean(z, axis=-1, keepdims=True)
        var = jnp.mean(jnp.square(z - mu), axis=-1, keepdims=True)
        y_ref[...] = (z - mu) * lax.rsqrt(var + EPS) * g2_ref[...] + b2_ref[...]

    @pl.when(i == 0)
    def _():
        issue(pos_ref, 0, False)

    for par in range(2):
        @pl.when((i & 1) == par)
        def _():
            wait(par)
            issue(posn_ref, 1 - par, True)
            combine(par)

            @pl.when(i == n - 1)
            def _():
                wait(1 - par)


def _final(pos, x1, wts, ys, wsg, wsu, wsd, g2, b2, *, tile, off, n, alpha):
    d = x1.shape[1]
    nt = n // tile
    kern = functools.partial(_final_kernel, alpha=alpha)
    return pl.pallas_call(
        kern,
        grid=(nt,),
        in_specs=[pl.BlockSpec((TOP_K, tile), lambda i: (0, i + off), memory_space=pltpu.SMEM),
                  pl.BlockSpec((TOP_K, tile), lambda i: (0, jnp.minimum(i + 1, nt - 1) + off),
                               memory_space=pltpu.SMEM),
                  pl.BlockSpec((tile, d), lambda i: (i + off, 0)),
                  pl.BlockSpec((TOP_K, tile), lambda i: (0, i + off)),
                  pl.BlockSpec(memory_space=pl.ANY),
                  _full(wsg.shape), _full(wsu.shape), _full(wsd.shape), _full(g2.shape), _full(b2.shape)],
        out_specs=pl.BlockSpec((tile, d), lambda i: (i, 0)),
        out_shape=jax.ShapeDtypeStruct((n, d), F32),
        scratch_shapes=[pltpu.VMEM((2, TOP_K, tile * PACK_ROWS, LANES), jnp.uint32),
                        pltpu.SemaphoreType.DMA((2,))],
        compiler_params=_cparams(("arbitrary",)),
        name="final",
    )(pos, pos, x1, wts, ys, wsg, wsu, wsd, g2, b2)


def _rope_tables(pos, qk_rope, nh):
    half = qk_rope // 2
    inv = ROPE_BASE ** (-jnp.arange(half, dtype=F32) / half)
    ang = pos.astype(F32)[:, None] * inv
    cos, sin = jnp.cos(ang), jnp.sin(ang)
    c = jnp.concatenate([cos, cos], axis=1)
    s = jnp.concatenate([-sin, sin], axis=1)
    rep = LANES // qk_rope
    rope_k = jnp.concatenate([jnp.tile(c, (1, rep)), jnp.tile(s, (1, rep))], axis=1)
    rope_q = jnp.concatenate([jnp.tile(c, (1, nh)), jnp.tile(s, (1, nh))], axis=1)
    return rope_k, rope_q


def _swap_halves(w, axis):
    a, b = jnp.split(w, 2, axis=axis)
    return jnp.concatenate([b, a], axis=axis)


def _block_diag_pairs(w):
    nh, r, c = w.shape
    z = jnp.zeros((nh // 2, r, c), w.dtype)
    top = jnp.concatenate([w[0::2], z], axis=2)
    bot = jnp.concatenate([z, w[1::2]], axis=2)
    return jnp.concatenate([top, bot], axis=1)


def kernel(x_prompt, x_sample, cache_mla_latent, cache_mla_krope, cache_diff_k, cache_diff_v, w_in, b_gate, g_q_norm, w_uq, w_uk, g_kv_norm, w_uv, w_o_mla, lambda_q1, lambda_k1, lambda_q2, lambda_k2, g_subln, w_o_diff, w_out, g_ln1, b_ln1, w_router, router_bias, w_exp_gate, w_exp_up, w_exp_down, w_sh_gate, w_sh_up, w_sh_down, g_ln2, b_ln2):
    depth = w_in.shape[0]
    assert depth == 1
    b, s, d = x_prompt.shape
    bs, ss, _ = x_sample.shape
    past = cache_mla_latent.shape[2]
    q_lora = g_q_norm.shape[1]
    kv_lora = g_kv_norm.shape[1]
    qk_rope = cache_mla_krope.shape[3]
    nh = w_uq.shape[2]
    qk_nope = w_uq.shape[3] - qk_rope
    mla_v = w_uv.shape[3]
    dnh, _, hd = cache_diff_k.shape[3:]
    diff_w = dnh * 2 * hd
    ne = w_router.shape[2]
    alpha = (2 * depth) ** 0.25
    lam_init = 0.8 - 0.6 * math.exp(-0.3 * 0)
    assert 2 * hd == LANES and kv_lora == LANES and 2 * qk_nope == LANES and LANES % qk_rope == 0

    wi = w_in[0]
    c0 = q_lora + kv_lora
    w_cq, w_ckv = wi[:, :q_lora], wi[:, q_lora:c0]
    w_kr = wi[:, c0:c0 + qk_rope]
    c1 = c0 + qk_rope
    w_d = wi[:, c1:c1 + 3 * diff_w]
    w_gates = wi[:, c1 + 3 * diff_w:]
    rep = LANES // qk_rope
    w1 = jnp.concatenate([w_cq, w_ckv, w_d, jnp.tile(w_kr, (1, rep)),
                          jnp.tile(_swap_halves(w_kr, 1), (1, rep))], axis=1).astype(BF16)
    uq = w_uq[0]
    wqn = uq[:, :, :qk_nope].reshape(q_lora, nh * qk_nope).astype(BF16)
    wqr = uq[:, :, qk_nope:].reshape(q_lora, nh * qk_rope).astype(BF16)
    wqs = _swap_halves(uq[:, :, qk_nope:], 2).reshape(q_lora, nh * qk_rope).astype(BF16)
    wuk = _block_diag_pairs(jnp.transpose(w_uk[0], (1, 2, 0))).astype(BF16)
    wuv = _block_diag_pairs(jnp.transpose(w_uv[0], (1, 0, 2))).astype(BF16)
    lam4 = jnp.concatenate([lambda_q1, lambda_k1, lambda_q2, lambda_k2], axis=0)
    merge_w = (w_gates.astype(BF16), b_gate, w_o_mla[0].astype(BF16), w_o_diff[0].astype(BF16),
               w_out[0].astype(BF16), g_ln1, b_ln1, jnp.transpose(w_router[0]).astype(BF16),
               jnp.transpose(router_bias))
    mla_scale = (qk_nope + qk_rope) ** -0.5
    dims = (q_lora, kv_lora, diff_w, qk_rope, hd ** -0.5 * LOG2E)

    n_p = b * s
    rope_k, rope_q = _rope_tables(jnp.arange(s, dtype=I32), qk_rope, nh)
    cq, lat, kr, kcat, dq, dk, dkb, dv, dvb = _proj(
        x_prompt.reshape(n_p, d), w1, g_q_norm, g_kv_norm, rope_k, tile=PROJ_TILE, dims=dims)
    r3 = lambda a: a.reshape(b, s, a.shape[1])
    oa = _mla(r3(cq), rope_q, r3(kcat), wqn, wqr, wqs, wuk, wuv, qb=ATT_BLOCK, kb=ATT_BLOCK, nh=nh,
              qk_rope=qk_rope, past=0, n_valid=s, scale=mla_scale, causal=True)
    ob = _diff(r3(dq), r3(dkb), r3(dvb), lam4, g_subln, qb=ATT_BLOCK, kb=ATT_BLOCK, nh=dnh, hd=hd,
               past=0, n_valid=s, causal=True, lam_init=lam_init)

    n_s = bs * ss
    sk = past + ss
    skp = -(-sk // LANES) * LANES
    rope_ks, rope_qs = _rope_tables(past + jnp.arange(ss, dtype=I32), qk_rope, nh)
    cq_s, lat_s, kr_s, kcat_s, dq_s, dk_s, dkb_s, dv_s, dvb_s = _proj(
        x_sample.reshape(n_s, d), w1, g_q_norm, g_kv_norm, jnp.tile(rope_ks, (bs, 1)), tile=n_s, dims=dims)
    r3s = lambda a: a.reshape(bs, ss, a.shape[1])
    padk = lambda a: jnp.pad(a, ((0, 0), (0, skp - sk), (0, 0)))
    kcat_all = padk(jnp.concatenate(
        [jnp.concatenate([cache_mla_latent[0], jnp.tile(cache_mla_krope[0], (1, 1, rep))], axis=2).astype(BF16),
         r3s(kcat_s)], axis=1))
    dk_all = padk(jnp.concatenate([cache_diff_k[0].reshape(bs, past, diff_w).astype(BF16), r3s(dkb_s)], axis=1))
    dv_all = padk(jnp.concatenate([cache_diff_v[0].reshape(bs, past, diff_w).astype(BF16), r3s(dvb_s)], axis=1))
    oa_s = _mla(r3s(cq_s), rope_qs, kcat_all, wqn, wqr, wqs, wuk, wuv, qb=ss, kb=skp, nh=nh,
                qk_rope=qk_rope, past=past, n_valid=sk, scale=mla_scale, causal=False)
    ob_s = _diff(r3s(dq_s), dk_all, dv_all, lam4, g_subln, qb=ss, kb=skp, nh=dnh, hd=hd,
                 past=past, n_valid=sk, causal=False, lam_init=lam_init)

    ntot = n_p + n_s
    x1, x1p, idx, wts = _merge(x_prompt.reshape(n_p, d), x_sample.reshape(n_s, d), oa.reshape(n_p, -1),
                          oa_s.reshape(n_s, -1), ob.reshape(n_p, -1), ob_s.reshape(n_s, -1), merge_w,
                          tile=TOK_TILE, alpha=alpha, ne=ne)
    nblocks = -(-(ntot * TOP_K) // EXP_BLOCK) + ne
    nbp = -(-nblocks // LANES) * LANES
    pos, be, bnv, bnext, nused = _plan(idx, ne=ne, blk=EXP_BLOCK, nbp=nbp, tile=TOK_TILE)
    xs = _scatter(pos, x1p, rows=nblocks * EXP_BLOCK, tile=TOK_TILE)
    ys = _ffn(be[0, :nblocks], bnv[0, :nblocks], bnext[0, :nblocks], nused[0, :1], xs, w_exp_gate[0], w_exp_up[0], w_exp_down[0],
              blk=EXP_BLOCK)
    fin_w = (w_sh_gate[0].astype(BF16), w_sh_up[0].astype(BF16), w_sh_down[0].astype(BF16), g_ln2, b_ln2)
    y_p = _final(pos, x1, wts, ys, *fin_w, tile=TOK_TILE, off=0, n=n_p, alpha=alpha)
    y_s = _final(pos, x1, wts, ys, *fin_w, tile=TOK_TILE, off=n_p // TOK_TILE, n=n_s, alpha=alpha)

    st = lambda a, bb, sq, tail: a.reshape((1, bb, sq) + tail)
    return (y_p.reshape(b, s, d), y_s.reshape(bs, ss, d),
            st(lat, b, s, (kv_lora,)), st(kr, b, s, (qk_rope,)),
            st(dk, b, s, (dnh, 2, hd)), st(dv, b, s, (dnh, 2 * hd)),
            st(lat_s, bs, ss, (kv_lora,)), st(kr_s, bs, ss, (qk_rope,)),
            st(dk_s, bs, ss, (dnh, 2, hd)), st(dv_s, bs, ss, (dnh, 2 * hd)))
```

```python
import functools
import math

import jax
import jax.numpy as jnp
from jax import lax
from jax.experimental import pallas as pl
from jax.experimental.pallas import tpu as pltpu

F32 = jnp.float32
BF16 = jnp.bfloat16
I32 = jnp.int32

CHUNK = 64
ROPE_BASE = 10000.0
EPS = 1e-6
N_GROUPS = 8
TOPK_GROUPS = 4
TOP_K = 8
ROUTED_SCALE = 2.5
NEG = -1e30

LANES = 128
TOK_TILE = 256
PROJ_TILE = 512
ATT_BLOCK = 512
SLAB_ROWS = 256
EXP_BLOCK = 512
FFN_SLOTS = 4
VMEM_LIMIT = 56 * 1024 * 1024


def _cparams(sem):
    return pltpu.CompilerParams(dimension_semantics=sem, vmem_limit_bytes=VMEM_LIMIT)


def _full(shape):
    nd = len(shape)
    return pl.BlockSpec(shape, lambda *_: (0,) * nd)


def _proj_kernel(x_ref, w_ref, gq_ref, gkv_ref, rope_ref,
                 cq_ref, lat_ref, kr_ref, kcat_ref, dq_ref, dk_ref, dkb_ref, dv_ref, dvb_ref,
                 *, q_lora, kv_lora, diff_w, qk_rope, dq_scale):
    x = x_ref[...].astype(BF16)
    h = jnp.dot(x, w_ref[...], preferred_element_type=F32)
    o = 0
    cq = h[:, o:o + q_lora]
    o += q_lora
    cq = cq * lax.rsqrt(jnp.mean(cq * cq, axis=-1, keepdims=True) + EPS) * gq_ref[...]
    cq_ref[...] = cq.astype(BF16)
    ckv = h[:, o:o + kv_lora]
    o += kv_lora
    lat = ckv * lax.rsqrt(jnp.mean(ckv * ckv, axis=-1, keepdims=True) + EPS) * gkv_ref[...]
    lat_ref[...] = lat
    dq_ref[...] = (h[:, o:o + diff_w] * dq_scale).astype(BF16)
    o += diff_w
    dk = h[:, o:o + diff_w]
    o += diff_w
    dk_ref[...] = dk
    dkb_ref[...] = dk.astype(BF16)
    dv = h[:, o:o + diff_w]
    o += diff_w
    dv_ref[...] = dv
    dvb_ref[...] = dv.astype(BF16)
    krr = h[:, o:o + LANES] * rope_ref[:, 0:LANES] + h[:, o + LANES:o + 2 * LANES] * rope_ref[:, LANES:2 * LANES]
    kr_ref[...] = krr[:, 0:qk_rope]
    kcat_ref[...] = jnp.concatenate([lat, krr], axis=1).astype(BF16)


def _proj(x, w1, gq, gkv, rope_k, *, tile, dims):
    n, d = x.shape
    q_lora, kv_lora, diff_w, qk_rope, dq_scale = dims
    wcols = w1.shape[1]
    nrope = rope_k.shape[0] // tile
    row = lambda c: pl.BlockSpec((tile, c), lambda i: (i, 0))
    outs = [
        jax.ShapeDtypeStruct((n, q_lora), BF16),
        jax.ShapeDtypeStruct((n, kv_lora), F32),
        jax.ShapeDtypeStruct((n, qk_rope), F32),
        jax.ShapeDtypeStruct((n, 2 * LANES), BF16),
        jax.ShapeDtypeStruct((n, diff_w), BF16),
        jax.ShapeDtypeStruct((n, diff_w), F32),
        jax.ShapeDtypeStruct((n, diff_w), BF16),
        jax.ShapeDtypeStruct((n, diff_w), F32),
        jax.ShapeDtypeStruct((n, diff_w), BF16),
    ]
    return pl.pallas_call(
        functools.partial(_proj_kernel, q_lora=q_lora, kv_lora=kv_lora, diff_w=diff_w, qk_rope=qk_rope,
                          dq_scale=dq_scale),
        grid=(n // tile,),
        in_specs=[row(d), _full((d, wcols)), _full((1, q_lora)), _full((1, kv_lora)),
                  pl.BlockSpec((tile, 2 * LANES), lambda i: (i % nrope, 0))],
        out_specs=[row(q_lora), row(kv_lora), row(qk_rope), row(2 * LANES), row(diff_w), row(diff_w),
                   row(diff_w), row(diff_w), row(diff_w)],
        out_shape=outs,
        compiler_params=_cparams(("arbitrary",)),
        name="proj",
    )(x, w1, gq, gkv, rope_k)


LOG2E = math.log2(math.e)


def _softmax_step(s2, v, m_scr, accl_scr, r0, rows, kb):
    m_prev = m_scr[r0:r0 + rows, :]
    m_cur = jnp.max(s2, axis=1, keepdims=True).astype(BF16).astype(F32)
    m_next = jnp.maximum(m_prev, m_cur)
    mb = m_next.astype(BF16)
    p = jnp.exp2(s2.astype(BF16) - jnp.concatenate([mb] * (kb // LANES), axis=1))
    alpha = jnp.exp2(m_prev - m_next)
    m_scr[r0:r0 + rows, :] = m_next
    v1 = jnp.concatenate([v, jnp.ones(v.shape, BF16)], axis=1)
    pv = jnp.dot(p, v1, preferred_element_type=F32)
    accl_scr[r0:r0 + rows, :] = accl_scr[r0:r0 + rows, :] * jnp.concatenate([alpha, alpha], axis=1) + pv


def _visible(qb, kb, q0, k0, n_valid):
    qpos = q0 + lax.broadcasted_iota(I32, (qb, kb), 0)
    kpos = k0 + lax.broadcasted_iota(I32, (qb, kb), 1)
    sh = CHUNK.bit_length() - 1
    vis = (lax.shift_right_logical(kpos, sh) <= lax.shift_right_logical(qpos, sh)) & (kpos < n_valid)
    return vis, qpos, kpos


def _slab_rows(x, r0, rp, qb):
    if rp >= qb:
        return x[None]
    q0 = r0 % qb
    return x[q0:q0 + rp]


def _mask_slab(s2, vis, r0, rp, qb, w):
    if rp >= qb:
        return jnp.where(vis[None], s2.reshape(rp // qb, qb, w), NEG).reshape(rp, w)
    return jnp.where(_slab_rows(vis, r0, rp, qb), s2, NEG)


def _diag_width(r0, rp, qb, w):
    return w if rp >= qb else min(w, r0 % qb + rp)


def _key_blocks(i, kb, causal, block):
    if not causal:
        block(0, kb, True)
        return

    def body(j, c):
        block(pl.multiple_of(j * kb, kb), kb, False)
        return c

    lax.fori_loop(0, i, body, 0)
    block(pl.multiple_of(i * kb, kb), kb, True)


def _mla_kernel(cq_ref, ropeq_ref, kcat_ref, wqn_ref, wqr_ref, wqs_ref, wuk_ref, wuv_ref, o_ref,
                qcat_scr, m_scr, accl_scr, *, qb, kb, nh, qk_rope, past, n_valid, scale, causal):
    i = pl.program_id(1)
    rows = nh * qb
    c2 = scale * LOG2E
    cq = cq_ref[0]
    qn = jnp.dot(cq, wqn_ref[...], preferred_element_type=F32).astype(BF16)
    qr = jnp.dot(cq, wqr_ref[...], preferred_element_type=F32)
    qs = jnp.dot(cq, wqs_ref[...], preferred_element_type=F32)
    rw = nh * qk_rope
    rot = (qr * ropeq_ref[:, 0:rw] + qs * ropeq_ref[:, rw:2 * rw]) * c2
    lane = lax.broadcasted_iota(I32, (qb, LANES), 1)
    per = LANES // qk_rope
    for j in range(nh // 2):
        ql = jnp.dot(qn[:, LANES * j:LANES * (j + 1)], wuk_ref[j], preferred_element_type=F32) * c2
        for u in range(2):
            hh = 2 * j + u
            rblk = rot[:, LANES * (hh // per):LANES * (hh // per + 1)]
            lo = qk_rope * (hh % per)
            part2 = jnp.where((lane >= lo) & (lane < lo + qk_rope), rblk, 0.0)
            qcat_scr[hh * qb:(hh + 1) * qb, :] = jnp.concatenate(
                [ql[:, LANES * u:LANES * (u + 1)], part2], axis=1).astype(BF16)
    m_scr[...] = jnp.full(m_scr.shape, NEG, F32)
    accl_scr[...] = jnp.zeros(accl_scr.shape, F32)

    def block(k0, w, masked):
        kblk = kcat_ref[0, pl.ds(k0, w), :]
        if masked:
            vis, _, _ = _visible(qb, w, past + i * qb, k0, n_valid)
        rp = min(rows, SLAB_ROWS)
        for sl in range(rows // rp):
            r0 = sl * rp
            we = _diag_width(r0, rp, qb, w) if (masked and causal) else w
            s2 = lax.dot_general(qcat_scr[r0:r0 + rp, :], kblk[0:we], (((1,), (1,)), ((), ())),
                                 preferred_element_type=F32)
            if masked:
                s2 = _mask_slab(s2, vis[:, 0:we], r0, rp, qb, we)
            _softmax_step(s2, kblk[0:we, 0:LANES], m_scr, accl_scr, r0, rp, we)

    _key_blocks(i, kb, causal, block)

    o_lat = accl_scr[:, 0:LANES] / accl_scr[:, LANES:2 * LANES]
    for j in range(nh // 2):
        pair = jnp.concatenate([o_lat[(2 * j) * qb:(2 * j + 1) * qb, :],
                                o_lat[(2 * j + 1) * qb:(2 * j + 2) * qb, :]], axis=1).astype(BF16)
        o_ref[0, :, LANES * j:LANES * (j + 1)] = jnp.dot(
            pair, wuv_ref[j], preferred_element_type=F32).astype(BF16)


def _mla(cq, rope_q, kcat, wqn, wqr, wqs, wuk, wuv, *, qb, kb, nh, qk_rope, past, n_valid, scale, causal):
    b, sq, q_lora = cq.shape
    assert not causal or (kb == qb and past == 0)
    sk = kcat.shape[1]
    nq = sq // qb
    rows = nh * qb
    ov = wuv.shape[0] * wuv.shape[2]
    kern = functools.partial(_mla_kernel, qb=qb, kb=kb, nh=nh, qk_rope=qk_rope, past=past,
                             n_valid=n_valid, scale=scale, causal=causal)
    return pl.pallas_call(
        kern,
        grid=(b, nq),
        in_specs=[pl.BlockSpec((1, qb, q_lora), lambda bb, i: (bb, i, 0)),
                  pl.BlockSpec((qb, rope_q.shape[1]), lambda bb, i: (i, 0)),
                  pl.BlockSpec((1, sk, kcat.shape[2]), lambda bb, i: (bb, 0, 0)),
                  _full(wqn.shape), _full(wqr.shape), _full(wqs.shape), _full(wuk.shape), _full(wuv.shape)],
        out_specs=pl.BlockSpec((1, qb, ov), lambda bb, i: (bb, i, 0)),
        out_shape=jax.ShapeDtypeStruct((b, sq, ov), BF16),
        scratch_shapes=[pltpu.VMEM((rows, 2 * LANES), BF16), pltpu.VMEM((rows, LANES), F32),
                        pltpu.VMEM((rows, 2 * LANES), F32)],
        compiler_params=_cparams(("arbitrary", "arbitrary")),
        name="mla",
    )(cq, rope_q, kcat, wqn, wqr, wqs, wuk, wuv)


def _diff_kernel(dq_ref, dk_ref, dv_ref, lam_ref, gsub_ref, o_ref, qq_scr, m_scr, accl_scr,
                 *, qb, kb, nh, hd, past, n_valid, causal, lam_init):
    i = pl.program_id(1)
    q = dq_ref[0]
    lane = lax.broadcasted_iota(I32, (qb, LANES), 1)
    zero = jnp.zeros((), BF16)
    for h in range(nh):
        qh = q[:, LANES * h:LANES * (h + 1)]
        qq_scr[(2 * h) * qb:(2 * h + 1) * qb, :] = jnp.where(lane < hd, qh, zero)
        qq_scr[(2 * h + 1) * qb:(2 * h + 2) * qb, :] = jnp.where(lane >= hd, qh, zero)
    m_scr[...] = jnp.full(m_scr.shape, NEG, F32)
    accl_scr[...] = jnp.zeros(accl_scr.shape, F32)

    def block(k0, w, masked):
        vis, qpos, kpos = _visible(qb, w, past + i * qb, k0, n_valid)
        dist = jnp.abs(qpos - kpos).astype(F32)
        for h in range(nh):
            slope2 = 2.0 ** (-8.0 * (h + 1) / nh) * LOG2E
            kh = dk_ref[0, pl.ds(k0, w), LANES * h:LANES * (h + 1)]
            vh = dv_ref[0, pl.ds(k0, w), LANES * h:LANES * (h + 1)]
            bias = slope2 * dist
            rp = min(2 * qb, SLAB_ROWS)
            for sl in range(2 * qb // rp):
                r0 = 2 * h * qb + sl * rp
                we = _diag_width(r0, rp, qb, w) if (masked and causal) else w
                s2 = lax.dot_general(qq_scr[r0:r0 + rp, :], kh[0:we], (((1,), (1,)), ((), ())),
                                     preferred_element_type=F32)
                if rp >= qb:
                    s2 = (s2.reshape(rp // qb, qb, we) - bias[None, :, 0:we]).reshape(rp, we)
                else:
                    s2 = s2 - _slab_rows(bias, r0, rp, qb)[:, 0:we]
                if masked:
                    s2 = _mask_slab(s2, vis[:, 0:we], r0, rp, qb, we)
                _softmax_step(s2, vh[0:we], m_scr, accl_scr, r0, rp, we)

    _key_blocks(i, kb, causal, block)

    lam1 = jnp.sum(lam_ref[0:1, :] * lam_ref[1:2, :], axis=1, keepdims=True)
    lam2 = jnp.sum(lam_ref[2:3, :] * lam_ref[3:4, :], axis=1, keepdims=True)
    lam = jnp.exp(lam1) - jnp.exp(lam2) + lam_init
    o_all = accl_scr[:, 0:LANES] / accl_scr[:, LANES:2 * LANES]
    for h in range(nh):
        o = o_all[(2 * h) * qb:(2 * h + 1) * qb, :] - lam * o_all[(2 * h + 1) * qb:(2 * h + 2) * qb, :]
        o = o * lax.rsqrt(jnp.mean(o * o, axis=-1, keepdims=True) + EPS) * gsub_ref[...]
        o_ref[0, :, LANES * h:LANES * (h + 1)] = (o * (1.0 - lam_init)).astype(BF16)


def _diff(dq, dk, dv, lam4, gsub, *, qb, kb, nh, hd, past, n_valid, causal, lam_init):
    b, sq, w = dq.shape
    assert not causal or (kb == qb and past == 0)
    sk = dk.shape[1]
    nq = sq // qb
    rows = 2 * nh * qb
    kern = functools.partial(_diff_kernel, qb=qb, kb=kb, nh=nh, hd=hd, past=past, n_valid=n_valid,
                             causal=causal, lam_init=lam_init)
    return pl.pallas_call(
        kern,
        grid=(b, nq),
        in_specs=[pl.BlockSpec((1, qb, w), lambda bb, i: (bb, i, 0)),
                  pl.BlockSpec((1, sk, w), lambda bb, i: (bb, 0, 0)),
                  pl.BlockSpec((1, sk, w), lambda bb, i: (bb, 0, 0)),
                  _full(lam4.shape), _full(gsub.shape)],
        out_specs=pl.BlockSpec((1, qb, w), lambda bb, i: (bb, i, 0)),
        out_shape=jax.ShapeDtypeStruct((b, sq, w), BF16),
        scratch_shapes=[pltpu.VMEM((rows, LANES), BF16), pltpu.VMEM((rows, LANES), F32),
                        pltpu.VMEM((rows, 2 * LANES), F32)],
        compiler_params=_cparams(("arbitrary", "arbitrary")),
        name="diffattn",
    )(dq, dk, dv, lam4, gsub)


def _red2(x, fn):
    return fn(fn(x, axis=0, keepdims=True), axis=1, keepdims=True)


PACK_ROWS = 4


def _store_packed(ref, x):
    rows, d = x.shape
    bits = pltpu.bitcast(x.astype(BF16).astype(F32), jnp.uint32)
    packed = bits[:, 0:d // 2] | (bits[:, d // 2:d] >> 16)
    for j in range(PACK_ROWS):
        ref[pl.ds(j, rows, stride=PACK_ROWS), :] = packed[:, LANES * j:LANES * (j + 1)]


def _load_packed(ref, rows):
    words = [ref[pl.ds(j, rows, stride=PACK_ROWS), :] for j in range(PACK_ROWS)]
    hi = [pltpu.bitcast(w & jnp.uint32(0xFFFF0000), F32) for w in words]
    lo = [pltpu.bitcast(w << 16, F32) for w in words]
    return jnp.concatenate(hi + lo, axis=1)


def _merge_kernel(xp_ref, xs_ref, oap_ref, oas_ref, obp_ref, obs_ref, wg_ref, bg_ref, woa_ref, wob_ref,
                  wout_ref, g1_ref, b1_ref, wrt_ref, rb_ref, x1_ref, x1p_ref, idx_ref, wts_ref, *, alpha, d, ne, ntp):
    is_p = pl.program_id(0) < ntp
    x = jnp.where(is_p, xp_ref[...], xs_ref[...])
    oa = jnp.where(is_p, oap_ref[...], oas_ref[...])
    ob = jnp.where(is_p, obp_ref[...], obs_ref[...])
    gates = jnp.dot(x.astype(BF16), wg_ref[...], preferred_element_type=F32) + bg_ref[...]
    g = jax.nn.sigmoid(gates)
    br_a = jnp.dot(oa, woa_ref[...], preferred_element_type=F32)
    br_b = jnp.dot(ob, wob_ref[...], preferred_element_type=F32)
    mixin = g[:, 0:d] * br_a + g[:, d:2 * d] * br_b
    mix = jnp.dot(mixin.astype(BF16), wout_ref[...], preferred_element_type=F32)
    z = alpha * x + mix
    mu = jnp.mean(z, axis=-1, keepdims=True)
    var = jnp.mean(jnp.square(z - mu), axis=-1, keepdims=True)
    x1 = (z - mu) * lax.rsqrt(var + EPS) * g1_ref[...] + b1_ref[...]
    x1_ref[...] = x1
    _store_packed(x1p_ref, x1)

    tm = x.shape[0]
    gsz = ne // N_GROUPS
    logits = lax.dot_general(wrt_ref[...], x1.astype(BF16), (((1,), (1,)), ((), ())),
                             preferred_element_type=F32)
    scores = jax.nn.sigmoid(logits)
    choice = scores + rb_ref[...]
    c3 = choice.reshape(N_GROUPS, gsz, tm)
    s3 = scores.reshape(N_GROUPS, gsz, tm)
    io = lax.broadcasted_iota(I32, (N_GROUPS, gsz, tm), 1)
    gio = lax.broadcasted_iota(I32, (N_GROUPS, gsz, tm), 0)
    eio = gio * gsz + io
    ninf = -jnp.inf
    m1 = jnp.max(c3, axis=1, keepdims=True)
    i1 = jnp.min(jnp.where(c3 == m1, io, gsz), axis=1, keepdims=True)
    m2 = jnp.max(jnp.where(io == i1, ninf, c3), axis=1, keepdims=True)
    gs = m1 + m2
    gio1 = lax.broadcasted_iota(I32, (N_GROUPS, 1, tm), 0)
    gsel = jnp.zeros((N_GROUPS, 1, tm), F32)
    cur = gs
    for _ in range(TOPK_GROUPS):
        gm = jnp.max(cur, axis=0, keepdims=True)
        gi = jnp.min(jnp.where(cur == gm, gio1, N_GROUPS), axis=0, keepdims=True)
        hit = gio1 == gi
        gsel = jnp.where(hit, 1.0, gsel)
        cur = jnp.where(hit, ninf, cur)
    cur = jnp.where(gsel > 0.5, c3, ninf)
    idx_rows, w_rows = [], []
    for _ in range(TOP_K):
        m = _red2(cur, jnp.max)
        ik = _red2(jnp.where(cur == m, eio, ne), jnp.min)
        hit = eio == ik
        w_rows.append(_red2(jnp.where(hit, s3, 0.0), jnp.sum)[0])
        idx_rows.append(ik[0])
        cur = jnp.where(hit, ninf, cur)
    wsum = w_rows[0]
    for w in w_rows[1:]:
        wsum = wsum + w
    for k in range(TOP_K):
        idx_ref[k:k + 1, :] = idx_rows[k]
        wts_ref[k:k + 1, :] = w_rows[k] / wsum * ROUTED_SCALE


def _merge(xp, xs, oap, oas, obp, obs, weights, *, tile, alpha, ne):
    n_p, d = xp.shape
    n_s = xs.shape[0]
    ntp, nts = n_p // tile, n_s // tile
    ntot = n_p + n_s
    prow = lambda c: pl.BlockSpec((tile, c), lambda i: (jnp.minimum(i, ntp - 1), 0))
    srow = lambda c: pl.BlockSpec((tile, c), lambda i: (jnp.maximum(i - ntp, 0), 0))
    wa, wb = oap.shape[1], obp.shape[1]
    return pl.pallas_call(
        functools.partial(_merge_kernel, alpha=alpha, d=d, ne=ne, ntp=ntp),
        grid=(ntp + nts,),
        in_specs=[prow(d), srow(d), prow(wa), srow(wa), prow(wb), srow(wb)] + [_full(a.shape) for a in weights],
        out_specs=[pl.BlockSpec((tile, d), lambda i: (i, 0)),
                   pl.BlockSpec((tile * PACK_ROWS, LANES), lambda i: (i, 0)),
                   pl.BlockSpec((TOP_K, tile), lambda i: (0, i)),
                   pl.BlockSpec((TOP_K, tile), lambda i: (0, i))],
        out_shape=[jax.ShapeDtypeStruct((ntot, d), F32),
                   jax.ShapeDtypeStruct((ntot * PACK_ROWS, LANES), jnp.uint32),
                   jax.ShapeDtypeStruct((TOP_K, ntot), I32),
                   jax.ShapeDtypeStruct((TOP_K, ntot), F32)],
        compiler_params=_cparams(("arbitrary",)),
        name="merge",
    )(xp, xs, oap, oas, obp, obs, *weights)


def _plan_kernel(idx_ref, ltri_ref, su_ref, pos_ref, be_ref, bnv_ref, bnext_ref, nused_ref, cnt_scr, base_scr,
                 *, ne, blk, nbp, tile):
    nt = idx_ref.shape[1] // tile
    eio = lax.broadcasted_iota(I32, (ne, tile), 0)

    def tile_hits(i):
        c0 = pl.multiple_of(i * tile, tile)
        idx = idx_ref[:, pl.ds(c0, tile)]
        hits = [eio == idx[k:k + 1, :] for k in range(TOP_K)]
        mh = jnp.zeros((ne, tile), F32)
        for hk in hits:
            mh = jnp.where(hk, 1.0, mh)
        return c0, hits, mh

    cnt_scr[...] = jnp.zeros(cnt_scr.shape, F32)

    def count(i, c):
        _, _, mh = tile_hits(i)
        cnt_scr[...] = cnt_scr[...] + jnp.sum(mh, axis=1, keepdims=True)
        return c

    lax.fori_loop(0, nt, count, 0)

    cnt = cnt_scr[...]
    sh = blk.bit_length() - 1
    nb = lax.shift_right_logical(cnt.astype(I32) + (blk - 1), sh)
    nbf = nb.astype(F32)
    start = jnp.dot(ltri_ref[...], jnp.broadcast_to(nbf, (ne, LANES)).astype(BF16),
                    preferred_element_type=F32)[:, 0:1]
    end = start + nbf
    base_scr[...] = start * float(blk)
    bio = lax.broadcasted_iota(I32, (ne, nbp), 1).astype(F32)
    be = jnp.sum(jnp.where(end <= bio, 1.0, 0.0), axis=0, keepdims=True)
    be_ref[...] = jnp.minimum(be, float(ne - 1)).astype(I32)
    inside = (start <= bio) & (bio < end)
    left = jnp.clip(cnt - (bio - start) * float(blk), 0.0, float(blk))
    bnv_ref[...] = jnp.sum(jnp.where(inside, left, 0.0), axis=0, keepdims=True).astype(I32)
    bnext_ref[...] = jnp.sum(jnp.where(inside, end, 0.0), axis=0, keepdims=True).astype(I32)
    nused_ref[...] = jnp.broadcast_to(jnp.sum(nbf, axis=0, keepdims=True), (1, LANES)).astype(I32)

    def place(i, c):
        c0, hits, mh = tile_hits(i)
        before = jnp.dot(mh.astype(BF16), su_ref[...], preferred_element_type=F32)
        rank = base_scr[...] + before
        for k in range(TOP_K):
            row = jnp.sum(jnp.where(hits[k], rank, 0.0), axis=0, keepdims=True).astype(I32)
            pos_ref[k:k + 1, pl.ds(c0, tile)] = row * PACK_ROWS
        base_scr[...] = base_scr[...] + jnp.sum(mh, axis=1, keepdims=True)
        return c

    lax.fori_loop(0, nt, place, 0)


def _plan(idx, *, ne, blk, nbp, tile):
    ntot = idx.shape[1]
    ltri = jnp.tril(jnp.ones((ne, ne), F32), -1).astype(BF16)
    su = jnp.triu(jnp.ones((tile, tile), F32), 1).astype(BF16)
    kern = functools.partial(_plan_kernel, ne=ne, blk=blk, nbp=nbp, tile=tile)
    return pl.pallas_call(
        kern,
        grid=(1,),
        in_specs=[_full((TOP_K, ntot)), _full((ne, ne)), _full((tile, tile))],
        out_specs=[_full((TOP_K, ntot)), _full((1, nbp)), _full((1, nbp)), _full((1, nbp)), _full((1, LANES))],
        out_shape=[jax.ShapeDtypeStruct((TOP_K, ntot), I32), jax.ShapeDtypeStruct((1, nbp), I32),
                   jax.ShapeDtypeStruct((1, nbp), I32), jax.ShapeDtypeStruct((1, nbp), I32),
                   jax.ShapeDtypeStruct((1, LANES), I32)],
        scratch_shapes=[pltpu.VMEM((ne, 1), F32), pltpu.VMEM((ne, 1), F32)],
        compiler_params=_cparams(("arbitrary",)),
        name="plan",
    )(idx, ltri, su)


def _scatter_kernel(pos_ref, x_ref, xs_ref, sem):
    tm = pos_ref.shape[1]

    for t in range(tm):
        src = x_ref.at[pl.ds(t * PACK_ROWS, PACK_ROWS), :]
        for k in range(TOP_K):
            p = pl.multiple_of(pos_ref[k, t], PACK_ROWS)
            pltpu.make_async_copy(src, xs_ref.at[pl.ds(p, PACK_ROWS), :], sem).start(priority=k % 2)
    for _ in range(TOP_K):
        pltpu.make_async_copy(x_ref, xs_ref.at[pl.ds(0, tm * PACK_ROWS), :], sem).wait()


def _scatter(pos, x1p, *, rows, tile):
    ntot = pos.shape[1]
    return pl.pallas_call(
        _scatter_kernel,
        grid=(ntot // tile,),
        in_specs=[pl.BlockSpec((TOP_K, tile), lambda i: (0, i), memory_space=pltpu.SMEM),
                  pl.BlockSpec((tile * PACK_ROWS, LANES), lambda i: (i, 0))],
        out_specs=pl.BlockSpec(memory_space=pl.ANY),
        out_shape=jax.ShapeDtypeStruct((rows * PACK_ROWS, LANES), jnp.uint32),
        scratch_shapes=[pltpu.SemaphoreType.DMA],
        compiler_params=_cparams(("arbitrary",)),
        name="scatter",
    )(pos, x1p)


def _ffn_kernel(be_ref, bnv_ref, bnext_ref, nused_ref, xs_ref, wg_ref, wu_ref, wd_ref, ys_ref,
                xbuf, ybuf, wgf, wuf, wdf, wgb, wub, wdb, xsem, ysem, wsem, *, blk):
    nused = nused_ref[0]
    prow = blk * PACK_ROWS

    def x_copy(g, slot):
        return pltpu.make_async_copy(xs_ref.at[pl.ds(pl.multiple_of(g * prow, prow), prow), :],
                                     xbuf.at[slot], xsem.at[slot])

    def y_copy(g, slot):
        return pltpu.make_async_copy(ybuf.at[slot], ys_ref.at[pl.ds(pl.multiple_of(g * prow, prow), prow), :],
                                     ysem.at[slot])

    def w_copies(e, slot):
        return (pltpu.make_async_copy(wg_ref.at[e], wgf.at[slot], wsem.at[slot]),
                pltpu.make_async_copy(wu_ref.at[e], wuf.at[slot], wsem.at[slot]),
                pltpu.make_async_copy(wd_ref.at[e], wdf.at[slot], wsem.at[slot]))

    @pl.when(nused > 0)
    def _():
        for c in w_copies(be_ref[0], 0):
            c.start()

    for j in range(FFN_SLOTS - 1):
        @pl.when(j < nused)
        def _():
            x_copy(j, j).start()

    def body(g, ws):
        e = be_ref[g]
        first = (g == 0) | (be_ref[jnp.maximum(g - 1, 0)] != e)
        ws = jnp.where(first & (g > 0), 1 - ws, ws)

        @pl.when(first)
        def _():
            for c in w_copies(e, ws):
                c.wait()
            wgb[...] = wgf[ws].astype(BF16)
            wub[...] = wuf[ws].astype(BF16)
            wdb[...] = wdf[ws].astype(BF16)
            gn = bnext_ref[g]

            @pl.when(gn < nused)
            def _():
                for c in w_copies(be_ref[jnp.minimum(gn, nused - 1)], 1 - ws):
                    c.start()

        slot = g & (FFN_SLOTS - 1)
        ahead = g + (FFN_SLOTS - 1)

        @pl.when(ahead < nused)
        def _():
            x_copy(ahead, ahead & (FFN_SLOTS - 1)).start()

        x_copy(g, slot).wait()
        rid = lax.broadcasted_iota(I32, (blk, 1), 0)
        x = jnp.where(rid < bnv_ref[g], _load_packed(xbuf.at[slot], blk), 0.0).astype(BF16)
        gg = jnp.dot(x, wgb[...], preferred_element_type=F32)
        uu = jnp.dot(x, wub[...], preferred_element_type=F32)
        h = (gg * jax.nn.sigmoid(gg)) * uu
        y = jnp.dot(h.astype(BF16), wdb[...], preferred_element_type=F32)

        @pl.when(g >= FFN_SLOTS)
        def _():
            y_copy(g - FFN_SLOTS, slot).wait()

        _store_packed(ybuf.at[slot], y)
        y_copy(g, slot).start()
        return ws

    lax.fori_loop(0, nused, body, jnp.int32(0))

    for j in range(FFN_SLOTS, 0, -1):
        @pl.when(nused >= j)
        def _():
            y_copy(nused - j, (nused - j) & (FFN_SLOTS - 1)).wait()


def _ffn(be, bnv, bnext, nused, xs, wg, wu, wd, *, blk):
    d, de = wg.shape[1], wg.shape[2]
    prow = blk * PACK_ROWS
    any_spec = pl.BlockSpec(memory_space=pl.ANY)
    grid_spec = pltpu.PrefetchScalarGridSpec(
        num_scalar_prefetch=4,
        grid=(1,),
        in_specs=[any_spec, any_spec, any_spec, any_spec],
        out_specs=any_spec,
        scratch_shapes=[pltpu.VMEM((FFN_SLOTS, prow, LANES), jnp.uint32),
                        pltpu.VMEM((FFN_SLOTS, prow, LANES), jnp.uint32),
                        pltpu.VMEM((2, d, de), F32), pltpu.VMEM((2, d, de), F32), pltpu.VMEM((2, de, d), F32),
                        pltpu.VMEM((d, de), BF16), pltpu.VMEM((d, de), BF16), pltpu.VMEM((de, d), BF16),
                        pltpu.SemaphoreType.DMA((FFN_SLOTS,)), pltpu.SemaphoreType.DMA((FFN_SLOTS,)),
                        pltpu.SemaphoreType.DMA((2,))],
    )
    return pl.pallas_call(
        functools.partial(_ffn_kernel, blk=blk),
        grid_spec=grid_spec,
        out_shape=jax.ShapeDtypeStruct(xs.shape, jnp.uint32),
        compiler_params=_cparams(("arbitrary",)),
        name="ffn",
    )(be, bnv, bnext, nused, xs, wg, wu, wd)


def _final_kernel(pos_ref, posn_ref, x1_ref, wts_ref, ys_ref, wsg_ref, wsu_ref, wsd_ref, g2_ref, b2_ref, y_ref,
                  gbuf, sem, *, alpha):
    i = pl.program_id(0)
    n = pl.num_programs(0)
    tm = x1_ref.shape[0]

    def gather(p_ref, slot, t, dst):
        for k in range(TOP_K):
            p = pl.multiple_of(p_ref[k, t], PACK_ROWS)
            pltpu.make_async_copy(ys_ref.at[pl.ds(p, PACK_ROWS), :], gbuf.at[slot, k, pl.ds(dst, PACK_ROWS), :],
                                  sem.at[slot]).start(priority=k % 2)

    def issue(p_ref, slot, unrolled):
        if unrolled:
            for t in range(tm):
                gather(p_ref, slot, t, t * PACK_ROWS)
        else:
            def tok(t, c):
                gather(p_ref, slot, t, pl.multiple_of(t * PACK_ROWS, PACK_ROWS))
                return c
            lax.fori_loop(0, tm, tok, 0)

    def wait(slot):
        for k in range(TOP_K):
            pltpu.make_async_copy(ys_ref.at[pl.ds(0, tm * PACK_ROWS), :], gbuf.at[slot, k], sem.at[slot]).wait()

    def combine(slot):
        x1 = x1_ref[...]
        xb = x1.astype(BF16)
        g = jnp.dot(xb, wsg_ref[...], preferred_element_type=F32)
        u = jnp.dot(xb, wsu_ref[...], preferred_element_type=F32)
        shared = jnp.dot(((g * jax.nn.sigmoid(g)) * u).astype(BF16), wsd_ref[...], preferred_element_type=F32)
        eye = lax.broadcasted_iota(I32, (tm, tm), 0) == lax.broadcasted_iota(I32, (tm, tm), 1)
        routed = jnp.zeros(x1.shape, F32)
        for k in range(TOP_K):
            wcol = jnp.sum(jnp.where(eye, wts_ref[k:k + 1, :], 0.0), axis=1, keepdims=True)
            routed = routed + _load_packed(gbuf.at[slot, k], tm) * wcol
        z = alpha * x1 + (routed + shared)
        mu = jnp.mean(z, axis=-1, keepdims=True)
        var = jnp.mean(jnp.square(z - mu), axis=-1, keepdims=True)
        y_ref[...] = (z - mu) * lax.rsqrt(var + EPS) * g2_ref[...] + b2_ref[...]

    @pl.when(i == 0)
    def _():
        issue(pos_ref, 0, False)

    for par in range(2):
        @pl.when((i & 1) == par)
        def _():
            wait(par)
            issue(posn_ref, 1 - par, True)
            combine(par)

            @pl.when(i == n - 1)
            def _():
                wait(1 - par)


def _final(pos, x1, wts, ys, wsg, wsu, wsd, g2, b2, *, tile, off, n, alpha):
    d = x1.shape[1]
    nt = n // tile
    kern = functools.partial(_final_kernel, alpha=alpha)
    return pl.pallas_call(
        kern,
        grid=(nt,),
        in_specs=[pl.BlockSpec((TOP_K, tile), lambda i: (0, i + off), memory_space=pltpu.SMEM),
                  pl.BlockSpec((TOP_K, tile), lambda i: (0, jnp.minimum(i + 1, nt - 1) + off),
                               memory_space=pltpu.SMEM),
                  pl.BlockSpec((tile, d), lambda i: (i + off, 0)),
                  pl.BlockSpec((TOP_K, tile), lambda i: (0, i + off)),
                  pl.BlockSpec(memory_space=pl.ANY),
                  _full(wsg.shape), _full(wsu.shape), _full(wsd.shape), _full(g2.shape), _full(b2.shape)],
        out_specs=pl.BlockSpec((tile, d), lambda i: (i, 0)),
        out_shape=jax.ShapeDtypeStruct((n, d), F32),
        scratch_shapes=[pltpu.VMEM((2, TOP_K, tile * PACK_ROWS, LANES), jnp.uint32),
                        pltpu.SemaphoreType.DMA((2,))],
        compiler_params=_cparams(("arbitrary",)),
        name="final",
    )(pos, pos, x1, wts, ys, wsg, wsu, wsd, g2, b2)


def _rope_tables(pos, qk_rope, nh):
    half = qk_rope // 2
    inv = ROPE_BASE ** (-jnp.arange(half, dtype=F32) / half)
    ang = pos.astype(F32)[:, None] * inv
    cos, sin = jnp.cos(ang), jnp.sin(ang)
    c = jnp.concatenate([cos, cos], axis=1)
    s = jnp.concatenate([-sin, sin], axis=1)
    rep = LANES // qk_rope
    rope_k = jnp.concatenate([jnp.tile(c, (1, rep)), jnp.tile(s, (1, rep))], axis=1)
    rope_q = jnp.concatenate([jnp.tile(c, (1, nh)), jnp.tile(s, (1, nh))], axis=1)
    return rope_k, rope_q


def _swap_halves(w, axis):
    a, b = jnp.split(w, 2, axis=axis)
    return jnp.concatenate([b, a], axis=axis)


def _block_diag_pairs(w):
    nh, r, c = w.shape
    z = jnp.zeros((nh // 2, r, c), w.dtype)
    top = jnp.concatenate([w[0::2], z], axis=2)
    bot = jnp.concatenate([z, w[1::2]], axis=2)
    return jnp.concatenate([top, bot], axis=1)


def kernel(x_prompt, x_sample, cache_mla_latent, cache_mla_krope, cache_diff_k, cache_diff_v, w_in, b_gate, g_q_norm, w_uq, w_uk, g_kv_norm, w_uv, w_o_mla, lambda_q1, lambda_k1, lambda_q2, lambda_k2, g_subln, w_o_diff, w_out, g_ln1, b_ln1, w_router, router_bias, w_exp_gate, w_exp_up, w_exp_down, w_sh_gate, w_sh_up, w_sh_down, g_ln2, b_ln2):
    depth = w_in.shape[0]
    assert depth == 1
    b, s, d = x_prompt.shape
    bs, ss, _ = x_sample.shape
    past = cache_mla_latent.shape[2]
    q_lora = g_q_norm.shape[1]
    kv_lora = g_kv_norm.shape[1]
    qk_rope = cache_mla_krope.shape[3]
    nh = w_uq.shape[2]
    qk_nope = w_uq.shape[3] - qk_rope
    mla_v = w_uv.shape[3]
    dnh, _, hd = cache_diff_k.shape[3:]
    diff_w = dnh * 2 * hd
    ne = w_router.shape[2]
    alpha = (2 * depth) ** 0.25
    lam_init = 0.8 - 0.6 * math.exp(-0.3 * 0)
    assert 2 * hd == LANES and kv_lora == LANES and 2 * qk_nope == LANES and LANES % qk_rope == 0

    wi = w_in[0]
    c0 = q_lora + kv_lora
    w_cq, w_ckv = wi[:, :q_lora], wi[:, q_lora:c0]
    w_kr = wi[:, c0:c0 + qk_rope]
    c1 = c0 + qk_rope
    w_d = wi[:, c1:c1 + 3 * diff_w]
    w_gates = wi[:, c1 + 3 * diff_w:]
    rep = LANES // qk_rope
    w1 = jnp.concatenate([w_cq, w_ckv, w_d, jnp.tile(w_kr, (1, rep)),
                          jnp.tile(_swap_halves(w_kr, 1), (1, rep))], axis=1).astype(BF16)
    uq = w_uq[0]
    wqn = uq[:, :, :qk_nope].reshape(q_lora, nh * qk_nope).astype(BF16)
    wqr = uq[:, :, qk_nope:].reshape(q_lora, nh * qk_rope).astype(BF16)
    wqs = _swap_halves(uq[:, :, qk_nope:], 2).reshape(q_lora, nh * qk_rope).astype(BF16)
    wuk = _block_diag_pairs(jnp.transpose(w_uk[0], (1, 2, 0))).astype(BF16)
    wuv = _block_diag_pairs(jnp.transpose(w_uv[0], (1, 0, 2))).astype(BF16)
    lam4 = jnp.concatenate([lambda_q1, lambda_k1, lambda_q2, lambda_k2], axis=0)
    merge_w = (w_gates.astype(BF16), b_gate, w_o_mla[0].astype(BF16), w_o_diff[0].astype(BF16),
               w_out[0].astype(BF16), g_ln1, b_ln1, jnp.transpose(w_router[0]).astype(BF16),
               jnp.transpose(router_bias))
    mla_scale = (qk_nope + qk_rope) ** -0.5
    dims = (q_lora, kv_lora, diff_w, qk_rope, hd ** -0.5 * LOG2E)

    n_p = b * s
    rope_k, rope_q = _rope_tables(jnp.arange(s, dtype=I32), qk_rope, nh)
    cq, lat, kr, kcat, dq, dk, dkb, dv, dvb = _proj(
        x_prompt.reshape(n_p, d), w1, g_q_norm, g_kv_norm, rope_k, tile=PROJ_TILE, dims=dims)
    r3 = lambda a: a.reshape(b, s, a.shape[1])
    oa = _mla(r3(cq), rope_q, r3(kcat), wqn, wqr, wqs, wuk, wuv, qb=ATT_BLOCK, kb=ATT_BLOCK, nh=nh,
              qk_rope=qk_rope, past=0, n_valid=s, scale=mla_scale, causal=True)
    ob = _diff(r3(dq), r3(dkb), r3(dvb), lam4, g_subln, qb=ATT_BLOCK, kb=ATT_BLOCK, nh=dnh, hd=hd,
               past=0, n_valid=s, causal=True, lam_init=lam_init)

    n_s = bs * ss
    sk = past + ss
    skp = -(-sk // LANES) * LANES
    rope_ks, rope_qs = _rope_tables(past + jnp.arange(ss, dtype=I32), qk_rope, nh)
    cq_s, lat_s, kr_s, kcat_s, dq_s, dk_s, dkb_s, dv_s, dvb_s = _proj(
        x_sample.reshape(n_s, d), w1, g_q_norm, g_kv_norm, jnp.tile(rope_ks, (bs, 1)), tile=n_s, dims=dims)
    r3s = lambda a: a.reshape(bs, ss, a.shape[1])
    padk = lambda a: jnp.pad(a, ((0, 0), (0, skp - sk), (0, 0)))
    kcat_all = padk(jnp.concatenate(
        [jnp.concatenate([cache_mla_latent[0], jnp.tile(cache_mla_krope[0], (1, 1, rep))], axis=2).astype(BF16),
         r3s(kcat_s)], axis=1))
    dk_all = padk(jnp.concatenate([cache_diff_k[0].reshape(bs, past, diff_w).astype(BF16), r3s(dkb_s)], axis=1))
    dv_all = padk(jnp.concatenate([cache_diff_v[0].reshape(bs, past, diff_w).astype(BF16), r3s(dvb_s)], axis=1))
    oa_s = _mla(r3s(cq_s), rope_qs, kcat_all, wqn, wqr, wqs, wuk, wuv, qb=ss, kb=skp, nh=nh,
                qk_rope=qk_rope, past=past, n_valid=sk, scale=mla_scale, causal=False)
    ob_s = _diff(r3s(dq_s), dk_all, dv_all, lam4, g_subln, qb=ss, kb=skp, nh=dnh, hd=hd,
                 past=past, n_valid=sk, causal=False, lam_init=lam_init)

    ntot = n_p + n_s
    x1, x1p, idx, wts = _merge(x_prompt.reshape(n_p, d), x_sample.reshape(n_s, d), oa.reshape(n_p, -1),
                          oa_s.reshape(n_s, -1), ob.reshape(n_p, -1), ob_s.reshape(n_s, -1), merge_w,
                          tile=TOK_TILE, alpha=alpha, ne=ne)
    nblocks = -(-(ntot * TOP_K) // EXP_BLOCK) + ne
    nbp = -(-nblocks // LANES) * LANES
    pos, be, bnv, bnext, nused = _plan(idx, ne=ne, blk=EXP_BLOCK, nbp=nbp, tile=TOK_TILE)
    xs = _scatter(pos, x1p, rows=nblocks * EXP_BLOCK, tile=TOK_TILE)
    ys = _ffn(be[0, :nblocks], bnv[0, :nblocks], bnext[0, :nblocks], nused[0, :1], xs, w_exp_gate[0], w_exp_up[0], w_exp_down[0],
              blk=EXP_BLOCK)
    fin_w = (w_sh_gate[0].astype(BF16), w_sh_up[0].astype(BF16), w_sh_down[0].astype(BF16), g_ln2, b_ln2)
    y_p = _final(pos, x1, wts, ys, *fin_w, tile=TOK_TILE, off=0, n=n_p, alpha=alpha)
    y_s = _final(pos, x1, wts, ys, *fin_w, tile=TOK_TILE, off=n_p // TOK_TILE, n=n_s, alpha=alpha)

    st = lambda a, bb, sq, tail: a.reshape((1, bb, sq) + tail)
    return (y_p.reshape(b, s, d), y_s.reshape(bs, ss, d),
            st(lat, b, s, (kv_lora,)), st(kr, b, s, (qk_rope,)),
            st(dk, b, s, (dnh, 2, hd)), st(dv, b, s, (dnh, 2 * hd)),
            st(lat_s, bs, ss, (kv_lora,)), st(kr_s, bs, ss, (qk_rope,)),
            st(dk_s, bs, ss, (dnh, 2, hd)), st(dv_s, bs, ss, (dnh, 2 * hd)))
```

```python
import functools
import math

import jax
import jax.numpy as jnp
from jax import lax
from jax.experimental import pallas as pl
from jax.experimental.pallas import tpu as pltpu

F32 = jnp.float32
BF16 = jnp.bfloat16
I32 = jnp.int32

CHUNK = 64
ROPE_BASE = 10000.0
EPS = 1e-6
N_GROUPS = 8
TOPK_GROUPS = 4
TOP_K = 8
ROUTED_SCALE = 2.5
NEG = -1e30

LANES = 128
TOK_TILE = 256
PROJ_TILE = 512
ATT_BLOCK = 512
SLAB_ROWS = 256
EXP_BLOCK = 512
FFN_SLOTS = 4
VMEM_LIMIT = 56 * 1024 * 1024


def _cparams(sem):
    return pltpu.CompilerParams(dimension_semantics=sem, vmem_limit_bytes=VMEM_LIMIT)


def _full(shape):
    nd = len(shape)
    return pl.BlockSpec(shape, lambda *_: (0,) * nd)


def _proj_kernel(x_ref, w_ref, gq_ref, gkv_ref, rope_ref,
                 cq_ref, lat_ref, kr_ref, kcat_ref, dq_ref, dk_ref, dkb_ref, dv_ref, dvb_ref,
                 *, q_lora, kv_lora, diff_w, qk_rope, dq_scale):
    x = x_ref[...].astype(BF16)
    h = jnp.dot(x, w_ref[...], preferred_element_type=F32)
    o = 0
    cq = h[:, o:o + q_lora]
    o += q_lora
    cq = cq * lax.rsqrt(jnp.mean(cq * cq, axis=-1, keepdims=True) + EPS) * gq_ref[...]
    cq_ref[...] = cq.astype(BF16)
    ckv = h[:, o:o + kv_lora]
    o += kv_lora
    lat = ckv * lax.rsqrt(jnp.mean(ckv * ckv, axis=-1, keepdims=True) + EPS) * gkv_ref[...]
    lat_ref[...] = lat
    dq_ref[...] = (h[:, o:o + diff_w] * dq_scale).astype(BF16)
    o += diff_w
    dk = h[:, o:o + diff_w]
    o += diff_w
    dk_ref[...] = dk
    dkb_ref[...] = dk.astype(BF16)
    dv = h[:, o:o + diff_w]
    o += diff_w
    dv_ref[...] = dv
    dvb_ref[...] = dv.astype(BF16)
    krr = h[:, o:o + LANES] * rope_ref[:, 0:LANES] + h[:, o + LANES:o + 2 * LANES] * rope_ref[:, LANES:2 * LANES]
    kr_ref[...] = krr[:, 0:qk_rope]
    kcat_ref[...] = jnp.concatenate([lat, krr], axis=1).astype(BF16)


def _proj(x, w1, gq, gkv, rope_k, *, tile, dims):
    n, d = x.shape
    q_lora, kv_lora, diff_w, qk_rope, dq_scale = dims
    wcols = w1.shape[1]
    nrope = rope_k.shape[0] // tile
    row = lambda c: pl.BlockSpec((tile, c), lambda i: (i, 0))
    outs = [
        jax.ShapeDtypeStruct((n, q_lora), BF16),
        jax.ShapeDtypeStruct((n, kv_lora), F32),
        jax.ShapeDtypeStruct((n, qk_rope), F32),
        jax.ShapeDtypeStruct((n, 2 * LANES), BF16),
        jax.ShapeDtypeStruct((n, diff_w), BF16),
        jax.ShapeDtypeStruct((n, diff_w), F32),
        jax.ShapeDtypeStruct((n, diff_w), BF16),
        jax.ShapeDtypeStruct((n, diff_w), F32),
        jax.ShapeDtypeStruct((n, diff_w), BF16),
    ]
    return pl.pallas_call(
        functools.partial(_proj_kernel, q_lora=q_lora, kv_lora=kv_lora, diff_w=diff_w, qk_rope=qk_rope,
                          dq_scale=dq_scale),
        grid=(n // tile,),
        in_specs=[row(d), _full((d, wcols)), _full((1, q_lora)), _full((1, kv_lora)),
                  pl.BlockSpec((tile, 2 * LANES), lambda i: (i % nrope, 0))],
        out_specs=[row(q_lora), row(kv_lora), row(qk_rope), row(2 * LANES), row(diff_w), row(diff_w),
                   row(diff_w), row(diff_w), row(diff_w)],
        out_shape=outs,
        compiler_params=_cparams(("arbitrary",)),
        name="proj",
    )(x, w1, gq, gkv, rope_k)


LOG2E = math.log2(math.e)


def _softmax_step(s2, v, m_scr, accl_scr, r0, rows, kb):
    m_prev = m_scr[r0:r0 + rows, :]
    m_cur = jnp.max(s2, axis=1, keepdims=True).astype(BF16).astype(F32)
    m_next = jnp.maximum(m_prev, m_cur)
    mb = m_next.astype(BF16)
    p = jnp.exp2(s2.astype(BF16) - jnp.concatenate([mb] * (kb // LANES), axis=1))
    alpha = jnp.exp2(m_prev - m_next)
    m_scr[r0:r0 + rows, :] = m_next
    v1 = jnp.concatenate([v, jnp.ones(v.shape, BF16)], axis=1)
    pv = jnp.dot(p, v1, preferred_element_type=F32)
    accl_scr[r0:r0 + rows, :] = accl_scr[r0:r0 + rows, :] * jnp.concatenate([alpha, alpha], axis=1) + pv


def _visible(qb, kb, q0, k0, n_valid):
    qpos = q0 + lax.broadcasted_iota(I32, (qb, kb), 0)
    kpos = k0 + lax.broadcasted_iota(I32, (qb, kb), 1)
    sh = CHUNK.bit_length() - 1
    vis = (lax.shift_right_logical(kpos, sh) <= lax.shift_right_logical(qpos, sh)) & (kpos < n_valid)
    return vis, qpos, kpos


def _slab_rows(x, r0, rp, qb):
    if rp >= qb:
        return x[None]
    q0 = r0 % qb
    return x[q0:q0 + rp]


def _mask_slab(s2, vis, r0, rp, qb, w):
    if rp >= qb:
        return jnp.where(vis[None], s2.reshape(rp // qb, qb, w), NEG).reshape(rp, w)
    return jnp.where(_slab_rows(vis, r0, rp, qb), s2, NEG)


def _key_blocks(i, kb, causal, block):
    if not causal:
        block(0, kb, True)
        return

    def body(j, c):
        block(pl.multiple_of(j * kb, kb), kb, False)
        return c

    lax.fori_loop(0, i, body, 0)
    block(pl.multiple_of(i * kb, kb), kb, True)


def _attn_kernel(cq_ref, ropeq_ref, kcat_ref, wqn_ref, wqr_ref, wqs_ref, wuk_ref, wuv_ref,
                 dq_ref, dk_ref, dv_ref, lam_ref, gsub_ref, oa_ref, ob_ref,
                 qcat_scr, ma_scr, accla_scr, qq_scr, mb_scr, acclb_scr,
                 *, qb, kb, nh, dnh, hd, qk_rope, past, n_valid, scale, causal, lam_init):
    i = pl.program_id(1)
    rows_a = nh * qb
    c2 = scale * LOG2E
    cq = cq_ref[0]
    qn = jnp.dot(cq, wqn_ref[...], preferred_element_type=F32).astype(BF16)
    qr = jnp.dot(cq, wqr_ref[...], preferred_element_type=F32)
    qs = jnp.dot(cq, wqs_ref[...], preferred_element_type=F32)
    rw = nh * qk_rope
    rot = (qr * ropeq_ref[:, 0:rw] + qs * ropeq_ref[:, rw:2 * rw]) * c2
    lane = lax.broadcasted_iota(I32, (qb, LANES), 1)
    per = LANES // qk_rope
    for j in range(nh // 2):
        ql = jnp.dot(qn[:, LANES * j:LANES * (j + 1)], wuk_ref[j], preferred_element_type=F32) * c2
        for u in range(2):
            hh = 2 * j + u
            rblk = rot[:, LANES * (hh // per):LANES * (hh // per + 1)]
            lo = qk_rope * (hh % per)
            part2 = jnp.where((lane >= lo) & (lane < lo + qk_rope), rblk, 0.0)
            qcat_scr[hh * qb:(hh + 1) * qb, :] = jnp.concatenate(
                [ql[:, LANES * u:LANES * (u + 1)], part2], axis=1).astype(BF16)
    q = dq_ref[0]
    zero = jnp.zeros((), BF16)
    for h in range(dnh):
        qh = q[:, LANES * h:LANES * (h + 1)]
        qq_scr[(2 * h) * qb:(2 * h + 1) * qb, :] = jnp.where(lane < hd, qh, zero)
        qq_scr[(2 * h + 1) * qb:(2 * h + 2) * qb, :] = jnp.where(lane >= hd, qh, zero)
    ma_scr[...] = jnp.full(ma_scr.shape, NEG, F32)
    accla_scr[...] = jnp.zeros(accla_scr.shape, F32)
    mb_scr[...] = jnp.full(mb_scr.shape, NEG, F32)
    acclb_scr[...] = jnp.zeros(acclb_scr.shape, F32)

    def block(k0, w, masked):
        vis, qpos, kpos = _visible(qb, w, past + i * qb, k0, n_valid)
        dist = jnp.abs(qpos - kpos).astype(F32)
        kblk = kcat_ref[0, pl.ds(k0, w), :]
        work_a, work_b = [], []
        rp = min(rows_a, SLAB_ROWS)
        for sl in range(rows_a // rp):
            def slab_a(r0=sl * rp):
                s2 = lax.dot_general(qcat_scr[r0:r0 + rp, :], kblk, (((1,), (1,)), ((), ())),
                                     preferred_element_type=F32)
                if masked:
                    s2 = _mask_slab(s2, vis, r0, rp, qb, w)
                _softmax_step(s2, kblk[:, 0:LANES], ma_scr, accla_scr, r0, rp, w)
            work_a.append(slab_a)
        rpb = min(2 * qb, SLAB_ROWS)
        for h in range(dnh):
            slope2 = 2.0 ** (-8.0 * (h + 1) / dnh) * LOG2E
            for sl in range(2 * qb // rpb):
                def slab_b(h=h, slope2=slope2, r0=2 * h * qb + sl * rpb):
                    kh = dk_ref[0, pl.ds(k0, w), LANES * h:LANES * (h + 1)]
                    vh = dv_ref[0, pl.ds(k0, w), LANES * h:LANES * (h + 1)]
                    bias = slope2 * dist
                    s2 = lax.dot_general(qq_scr[r0:r0 + rpb, :], kh, (((1,), (1,)), ((), ())),
                                         preferred_element_type=F32)
                    if rpb >= qb:
                        s2 = (s2.reshape(rpb // qb, qb, w) - bias[None]).reshape(rpb, w)
                    else:
                        s2 = s2 - _slab_rows(bias, r0, rpb, qb)
                    if masked:
                        s2 = _mask_slab(s2, vis, r0, rpb, qb, w)
                    _softmax_step(s2, vh, mb_scr, acclb_scr, r0, rpb, w)
                work_b.append(slab_b)
        for j in range(max(len(work_a), len(work_b))):
            if j < len(work_a):
                work_a[j]()
            if j < len(work_b):
                work_b[j]()

    _key_blocks(i, kb, causal, block)

    o_lat = accla_scr[:, 0:LANES] / accla_scr[:, LANES:2 * LANES]
    for j in range(nh // 2):
        pair = jnp.concatenate([o_lat[(2 * j) * qb:(2 * j + 1) * qb, :],
                                o_lat[(2 * j + 1) * qb:(2 * j + 2) * qb, :]], axis=1).astype(BF16)
        oa_ref[0, :, LANES * j:LANES * (j + 1)] = jnp.dot(
            pair, wuv_ref[j], preferred_element_type=F32).astype(BF16)
    lam1 = jnp.sum(lam_ref[0:1, :] * lam_ref[1:2, :], axis=1, keepdims=True)
    lam2 = jnp.sum(lam_ref[2:3, :] * lam_ref[3:4, :], axis=1, keepdims=True)
    lam = jnp.exp(lam1) - jnp.exp(lam2) + lam_init
    o_all = acclb_scr[:, 0:LANES] / acclb_scr[:, LANES:2 * LANES]
    for h in range(dnh):
        o = o_all[(2 * h) * qb:(2 * h + 1) * qb, :] - lam * o_all[(2 * h + 1) * qb:(2 * h + 2) * qb, :]
        o = o * lax.rsqrt(jnp.mean(o * o, axis=-1, keepdims=True) + EPS) * gsub_ref[...]
        ob_ref[0, :, LANES * h:LANES * (h + 1)] = (o * (1.0 - lam_init)).astype(BF16)


def _attn(cq, rope_q, kcat, wqn, wqr, wqs, wuk, wuv, dq, dk, dv, lam4, gsub,
          *, qb, kb, nh, dnh, hd, qk_rope, past, n_valid, scale, causal, lam_init):
    b, sq, q_lora = cq.shape
    sk = kcat.shape[1]
    w = dq.shape[2]
    nq = sq // qb
    rows_a, rows_b = nh * qb, 2 * dnh * qb
    ov = wuv.shape[0] * wuv.shape[2]
    kern = functools.partial(_attn_kernel, qb=qb, kb=kb, nh=nh, dnh=dnh, hd=hd, qk_rope=qk_rope, past=past,
                             n_valid=n_valid, scale=scale, causal=causal, lam_init=lam_init)
    qblk = lambda c: pl.BlockSpec((1, qb, c), lambda bb, i: (bb, i, 0))
    kblk = lambda c: pl.BlockSpec((1, sk, c), lambda bb, i: (bb, 0, 0))
    return pl.pallas_call(
        kern,
        grid=(b, nq),
        in_specs=[qblk(q_lora), pl.BlockSpec((qb, rope_q.shape[1]), lambda bb, i: (i, 0)), kblk(kcat.shape[2]),
                  _full(wqn.shape), _full(wqr.shape), _full(wqs.shape), _full(wuk.shape), _full(wuv.shape),
                  qblk(w), kblk(w), kblk(w), _full(lam4.shape), _full(gsub.shape)],
        out_specs=[qblk(ov), qblk(w)],
        out_shape=[jax.ShapeDtypeStruct((b, sq, ov), BF16), jax.ShapeDtypeStruct((b, sq, w), BF16)],
        scratch_shapes=[pltpu.VMEM((rows_a, 2 * LANES), BF16), pltpu.VMEM((rows_a, LANES), F32),
                        pltpu.VMEM((rows_a, 2 * LANES), F32),
                        pltpu.VMEM((rows_b, LANES), BF16), pltpu.VMEM((rows_b, LANES), F32),
                        pltpu.VMEM((rows_b, 2 * LANES), F32)],
        compiler_params=_cparams(("arbitrary", "arbitrary")),
        name="attn",
    )(cq, rope_q, kcat, wqn, wqr, wqs, wuk, wuv, dq, dk, dv, lam4, gsub)


def _red2(x, fn):
    return fn(fn(x, axis=0, keepdims=True), axis=1, keepdims=True)


PACK_ROWS = 4


def _store_packed(ref, x):
    rows, d = x.shape
    bits = pltpu.bitcast(x.astype(BF16).astype(F32), jnp.uint32)
    packed = bits[:, 0:d // 2] | (bits[:, d // 2:d] >> 16)
    for j in range(PACK_ROWS):
        ref[pl.ds(j, rows, stride=PACK_ROWS), :] = packed[:, LANES * j:LANES * (j + 1)]


def _load_packed(ref, rows):
    words = [ref[pl.ds(j, rows, stride=PACK_ROWS), :] for j in range(PACK_ROWS)]
    hi = [pltpu.bitcast(w & jnp.uint32(0xFFFF0000), F32) for w in words]
    lo = [pltpu.bitcast(w << 16, F32) for w in words]
    return jnp.concatenate(hi + lo, axis=1)


def _merge_kernel(xp_ref, xs_ref, oap_ref, oas_ref, obp_ref, obs_ref, wg_ref, bg_ref, woa_ref, wob_ref,
                  wout_ref, g1_ref, b1_ref, wrt_ref, rb_ref, x1_ref, x1p_ref, idx_ref, wts_ref, *, alpha, d, ne, ntp):
    is_p = pl.program_id(0) < ntp
    x = jnp.where(is_p, xp_ref[...], xs_ref[...])
    oa = jnp.where(is_p, oap_ref[...], oas_ref[...])
    ob = jnp.where(is_p, obp_ref[...], obs_ref[...])
    gates = jnp.dot(x.astype(BF16), wg_ref[...], preferred_element_type=F32) + bg_ref[...]
    g = jax.nn.sigmoid(gates)
    br_a = jnp.dot(oa, woa_ref[...], preferred_element_type=F32)
    br_b = jnp.dot(ob, wob_ref[...], preferred_element_type=F32)
    mixin = g[:, 0:d] * br_a + g[:, d:2 * d] * br_b
    mix = jnp.dot(mixin.astype(BF16), wout_ref[...], preferred_element_type=F32)
    z = alpha * x + mix
    mu = jnp.mean(z, axis=-1, keepdims=True)
    var = jnp.mean(jnp.square(z - mu), axis=-1, keepdims=True)
    x1 = (z - mu) * lax.rsqrt(var + EPS) * g1_ref[...] + b1_ref[...]
    x1_ref[...] = x1
    _store_packed(x1p_ref, x1)

    tm = x.shape[0]
    gsz = ne // N_GROUPS
    logits = lax.dot_general(wrt_ref[...], x1.astype(BF16), (((1,), (1,)), ((), ())),
                             preferred_element_type=F32)
    scores = jax.nn.sigmoid(logits)
    choice = scores + rb_ref[...]
    c3 = choice.reshape(N_GROUPS, gsz, tm)
    s3 = scores.reshape(N_GROUPS, gsz, tm)
    io = lax.broadcasted_iota(I32, (N_GROUPS, gsz, tm), 1)
    gio = lax.broadcasted_iota(I32, (N_GROUPS, gsz, tm), 0)
    eio = gio * gsz + io
    ninf = -jnp.inf
    m1 = jnp.max(c3, axis=1, keepdims=True)
    i1 = jnp.min(jnp.where(c3 == m1, io, gsz), axis=1, keepdims=True)
    m2 = jnp.max(jnp.where(io == i1, ninf, c3), axis=1, keepdims=True)
    gs = m1 + m2
    gio1 = lax.broadcasted_iota(I32, (N_GROUPS, 1, tm), 0)
    gsel = jnp.zeros((N_GROUPS, 1, tm), F32)
    cur = gs
    for _ in range(TOPK_GROUPS):
        gm = jnp.max(cur, axis=0, keepdims=True)
        gi = jnp.min(jnp.where(cur == gm, gio1, N_GROUPS), axis=0, keepdims=True)
        hit = gio1 == gi
        gsel = jnp.where(hit, 1.0, gsel)
        cur = jnp.where(hit, ninf, cur)
    cur = jnp.where(gsel > 0.5, c3, ninf)
    idx_rows, w_rows = [], []
    for _ in range(TOP_K):
        m = _red2(cur, jnp.max)
        ik = _red2(jnp.where(cur == m, eio, ne), jnp.min)
        hit = eio == ik
        w_rows.append(_red2(jnp.where(hit, s3, 0.0), jnp.sum)[0])
        idx_rows.append(ik[0])
        cur = jnp.where(hit, ninf, cur)
    wsum = w_rows[0]
    for w in w_rows[1:]:
        wsum = wsum + w
    for k in range(TOP_K):
        idx_ref[k:k + 1, :] = idx_rows[k]
        wts_ref[k:k + 1, :] = w_rows[k] / wsum * ROUTED_SCALE


def _merge(xp, xs, oap, oas, obp, obs, weights, *, tile, alpha, ne):
    n_p, d = xp.shape
    n_s = xs.shape[0]
    ntp, nts = n_p // tile, n_s // tile
    ntot = n_p + n_s
    prow = lambda c: pl.BlockSpec((tile, c), lambda i: (jnp.minimum(i, ntp - 1), 0))
    srow = lambda c: pl.BlockSpec((tile, c), lambda i: (jnp.maximum(i - ntp, 0), 0))
    wa, wb = oap.shape[1], obp.shape[1]
    return pl.pallas_call(
        functools.partial(_merge_kernel, alpha=alpha, d=d, ne=ne, ntp=ntp),
        grid=(ntp + nts,),
        in_specs=[prow(d), srow(d), prow(wa), srow(wa), prow(wb), srow(wb)] + [_full(a.shape) for a in weights],
        out_specs=[pl.BlockSpec((tile, d), lambda i: (i, 0)),
                   pl.BlockSpec((tile * PACK_ROWS, LANES), lambda i: (i, 0)),
                   pl.BlockSpec((TOP_K, tile), lambda i: (0, i)),
                   pl.BlockSpec((TOP_K, tile), lambda i: (0, i))],
        out_shape=[jax.ShapeDtypeStruct((ntot, d), F32),
                   jax.ShapeDtypeStruct((ntot * PACK_ROWS, LANES), jnp.uint32),
                   jax.ShapeDtypeStruct((TOP_K, ntot), I32),
                   jax.ShapeDtypeStruct((TOP_K, ntot), F32)],
        compiler_params=_cparams(("arbitrary",)),
        name="merge",
    )(xp, xs, oap, oas, obp, obs, *weights)


def _plan_kernel(idx_ref, ltri_ref, su_ref, pos_ref, be_ref, bnv_ref, bnext_ref, nused_ref, cnt_scr, base_scr,
                 *, ne, blk, nbp, tile):
    nt = idx_ref.shape[1] // tile
    eio = lax.broadcasted_iota(I32, (ne, tile), 0)

    def tile_hits(i):
        c0 = pl.multiple_of(i * tile, tile)
        idx = idx_ref[:, pl.ds(c0, tile)]
        hits = [eio == idx[k:k + 1, :] for k in range(TOP_K)]
        mh = jnp.zeros((ne, tile), F32)
        for hk in hits:
            mh = jnp.where(hk, 1.0, mh)
        return c0, hits, mh

    cnt_scr[...] = jnp.zeros(cnt_scr.shape, F32)

    def count(i, c):
        _, _, mh = tile_hits(i)
        cnt_scr[...] = cnt_scr[...] + jnp.sum(mh, axis=1, keepdims=True)
        return c

    lax.fori_loop(0, nt, count, 0)

    cnt = cnt_scr[...]
    sh = blk.bit_length() - 1
    nb = lax.shift_right_logical(cnt.astype(I32) + (blk - 1), sh)
    nbf = nb.astype(F32)
    start = jnp.dot(ltri_ref[...], jnp.broadcast_to(nbf, (ne, LANES)).astype(BF16),
                    preferred_element_type=F32)[:, 0:1]
    end = start + nbf
    base_scr[...] = start * float(blk)
    bio = lax.broadcasted_iota(I32, (ne, nbp), 1).astype(F32)
    be = jnp.sum(jnp.where(end <= bio, 1.0, 0.0), axis=0, keepdims=True)
    be_ref[...] = jnp.minimum(be, float(ne - 1)).astype(I32)
    inside = (start <= bio) & (bio < end)
    left = jnp.clip(cnt - (bio - start) * float(blk), 0.0, float(blk))
    bnv_ref[...] = jnp.sum(jnp.where(inside, left, 0.0), axis=0, keepdims=True).astype(I32)
    bnext_ref[...] = jnp.sum(jnp.where(inside, end, 0.0), axis=0, keepdims=True).astype(I32)
    nused_ref[...] = jnp.broadcast_to(jnp.sum(nbf, axis=0, keepdims=True), (1, LANES)).astype(I32)

    def place(i, c):
        c0, hits, mh = tile_hits(i)
        before = jnp.dot(mh.astype(BF16), su_ref[...], preferred_element_type=F32)
        rank = base_scr[...] + before
        for k in range(TOP_K):
            row = jnp.sum(jnp.where(hits[k], rank, 0.0), axis=0, keepdims=True).astype(I32)
            pos_ref[k:k + 1, pl.ds(c0, tile)] = row * PACK_ROWS
        base_scr[...] = base_scr[...] + jnp.sum(mh, axis=1, keepdims=True)
        return c

    lax.fori_loop(0, nt, place, 0)


def _plan(idx, *, ne, blk, nbp, tile):
    ntot = idx.shape[1]
    ltri = jnp.tril(jnp.ones((ne, ne), F32), -1).astype(BF16)
    su = jnp.triu(jnp.ones((tile, tile), F32), 1).astype(BF16)
    kern = functools.partial(_plan_kernel, ne=ne, blk=blk, nbp=nbp, tile=tile)
    return pl.pallas_call(
        kern,
        grid=(1,),
        in_specs=[_full((TOP_K, ntot)), _full((ne, ne)), _full((tile, tile))],
        out_specs=[_full((TOP_K, ntot)), _full((1, nbp)), _full((1, nbp)), _full((1, nbp)), _full((1, LANES))],
        out_shape=[jax.ShapeDtypeStruct((TOP_K, ntot), I32), jax.ShapeDtypeStruct((1, nbp), I32),
                   jax.ShapeDtypeStruct((1, nbp), I32), jax.ShapeDtypeStruct((1, nbp), I32),
                   jax.ShapeDtypeStruct((1, LANES), I32)],
        scratch_shapes=[pltpu.VMEM((ne, 1), F32), pltpu.VMEM((ne, 1), F32)],
        compiler_params=_cparams(("arbitrary",)),
        name="plan",
    )(idx, ltri, su)


def _scatter_kernel(pos_ref, x_ref, xs_ref, sem):
    tm = pos_ref.shape[1]

    for t in range(tm):
        src = x_ref.at[pl.ds(t * PACK_ROWS, PACK_ROWS), :]
        for k in range(TOP_K):
            p = pl.multiple_of(pos_ref[k, t], PACK_ROWS)
            pltpu.make_async_copy(src, xs_ref.at[pl.ds(p, PACK_ROWS), :], sem).start(priority=k % 2)
    for _ in range(TOP_K):
        pltpu.make_async_copy(x_ref, xs_ref.at[pl.ds(0, tm * PACK_ROWS), :], sem).wait()


def _scatter(pos, x1p, *, rows, tile):
    ntot = pos.shape[1]
    return pl.pallas_call(
        _scatter_kernel,
        grid=(ntot // tile,),
        in_specs=[pl.BlockSpec((TOP_K, tile), lambda i: (0, i), memory_space=pltpu.SMEM),
                  pl.BlockSpec((tile * PACK_ROWS, LANES), lambda i: (i, 0))],
        out_specs=pl.BlockSpec(memory_space=pl.ANY),
        out_shape=jax.ShapeDtypeStruct((rows * PACK_ROWS, LANES), jnp.uint32),
        scratch_shapes=[pltpu.SemaphoreType.DMA],
        compiler_params=_cparams(("arbitrary",)),
        name="scatter",
    )(pos, x1p)


def _ffn_kernel(be_ref, bnv_ref, bnext_ref, nused_ref, xs_ref, wg_ref, wu_ref, wd_ref, ys_ref,
                xbuf, ybuf, wgf, wuf, wdf, wgb, wub, wdb, xsem, ysem, wsem, *, blk):
    nused = nused_ref[0]
    prow = blk * PACK_ROWS

    def x_copy(g, slot):
        return pltpu.make_async_copy(xs_ref.at[pl.ds(pl.multiple_of(g * prow, prow), prow), :],
                                     xbuf.at[slot], xsem.at[slot])

    def y_copy(g, slot):
        return pltpu.make_async_copy(ybuf.at[slot], ys_ref.at[pl.ds(pl.multiple_of(g * prow, prow), prow), :],
                                     ysem.at[slot])

    def w_copies(e, slot):
        return (pltpu.make_async_copy(wg_ref.at[e], wgf.at[slot], wsem.at[slot]),
                pltpu.make_async_copy(wu_ref.at[e], wuf.at[slot], wsem.at[slot]),
                pltpu.make_async_copy(wd_ref.at[e], wdf.at[slot], wsem.at[slot]))

    @pl.when(nused > 0)
    def _():
        for c in w_copies(be_ref[0], 0):
            c.start()

    for j in range(FFN_SLOTS - 1):
        @pl.when(j < nused)
        def _():
            x_copy(j, j).start()

    def body(g, ws):
        e = be_ref[g]
        first = (g == 0) | (be_ref[jnp.maximum(g - 1, 0)] != e)
        ws = jnp.where(first & (g > 0), 1 - ws, ws)

        @pl.when(first)
        def _():
            for c in w_copies(e, ws):
                c.wait()
            wgb[...] = wgf[ws].astype(BF16)
            wub[...] = wuf[ws].astype(BF16)
            wdb[...] = wdf[ws].astype(BF16)
            gn = bnext_ref[g]

            @pl.when(gn < nused)
            def _():
                for c in w_copies(be_ref[jnp.minimum(gn, nused - 1)], 1 - ws):
                    c.start()

        slot = g & (FFN_SLOTS - 1)
        ahead = g + (FFN_SLOTS - 1)

        @pl.when(ahead < nused)
        def _():
            x_copy(ahead, ahead & (FFN_SLOTS - 1)).start()

        x_copy(g, slot).wait()
        rid = lax.broadcasted_iota(I32, (blk, 1), 0)
        x = jnp.where(rid < bnv_ref[g], _load_packed(xbuf.at[slot], blk), 0.0).astype(BF16)
        gg = jnp.dot(x, wgb[...], preferred_element_type=F32)
        uu = jnp.dot(x, wub[...], preferred_element_type=F32)
        h = (gg * jax.nn.sigmoid(gg)) * uu
        y = jnp.dot(h.astype(BF16), wdb[...], preferred_element_type=F32)

        @pl.when(g >= FFN_SLOTS)
        def _():
            y_copy(g - FFN_SLOTS, slot).wait()

        _store_packed(ybuf.at[slot], y)
        y_copy(g, slot).start()
        return ws

    lax.fori_loop(0, nused, body, jnp.int32(0))

    for j in range(FFN_SLOTS, 0, -1):
        @pl.when(nused >= j)
        def _():
            y_copy(nused - j, (nused - j) & (FFN_SLOTS - 1)).wait()


def _ffn(be, bnv, bnext, nused, xs, wg, wu, wd, *, blk):
    d, de = wg.shape[1], wg.shape[2]
    prow = blk * PACK_ROWS
    any_spec = pl.BlockSpec(memory_space=pl.ANY)
    grid_spec = pltpu.PrefetchScalarGridSpec(
        num_scalar_prefetch=4,
        grid=(1,),
        in_specs=[any_spec, any_spec, any_spec, any_spec],
        out_specs=any_spec,
        scratch_shapes=[pltpu.VMEM((FFN_SLOTS, prow, LANES), jnp.uint32),
                        pltpu.VMEM((FFN_SLOTS, prow, LANES), jnp.uint32),
                        pltpu.VMEM((2, d, de), F32), pltpu.VMEM((2, d, de), F32), pltpu.VMEM((2, de, d), F32),
                        pltpu.VMEM((d, de), BF16), pltpu.VMEM((d, de), BF16), pltpu.VMEM((de, d), BF16),
                        pltpu.SemaphoreType.DMA((FFN_SLOTS,)), pltpu.SemaphoreType.DMA((FFN_SLOTS,)),
                        pltpu.SemaphoreType.DMA((2,))],
    )
    return pl.pallas_call(
        functools.partial(_ffn_kernel, blk=blk),
        grid_spec=grid_spec,
        out_shape=jax.ShapeDtypeStruct(xs.shape, jnp.uint32),
        compiler_params=_cparams(("arbitrary",)),
        name="ffn",
    )(be, bnv, bnext, nused, xs, wg, wu, wd)


def _final_kernel(pos_ref, posn_ref, x1_ref, wts_ref, ys_ref, wsg_ref, wsu_ref, wsd_ref, g2_ref, b2_ref, y_ref,
                  gbuf, sem, *, alpha):
    i = pl.program_id(0)
    n = pl.num_programs(0)
    tm = x1_ref.shape[0]

    def gather(p_ref, slot, t, dst):
        for k in range(TOP_K):
            p = pl.multiple_of(p_ref[k, t], PACK_ROWS)
            pltpu.make_async_copy(ys_ref.at[pl.ds(p, PACK_ROWS), :], gbuf.at[slot, k, pl.ds(dst, PACK_ROWS), :],
                                  sem.at[slot]).start(priority=k % 2)

    def issue(p_ref, slot, unrolled):
        if unrolled:
            for t in range(tm):
                gather(p_ref, slot, t, t * PACK_ROWS)
        else:
            def tok(t, c):
                gather(p_ref, slot, t, pl.multiple_of(t * PACK_ROWS, PACK_ROWS))
                return c
            lax.fori_loop(0, tm, tok, 0)

    def wait(slot):
        for k in range(TOP_K):
            pltpu.make_async_copy(ys_ref.at[pl.ds(0, tm * PACK_ROWS), :], gbuf.at[slot, k], sem.at[slot]).wait()

    def combine(slot):
        x1 = x1_ref[...]
        xb = x1.astype(BF16)
        g = jnp.dot(xb, wsg_ref[...], preferred_element_type=F32)
        u = jnp.dot(xb, wsu_ref[...], preferred_element_type=F32)
        shared = jnp.dot(((g * jax.nn.sigmoid(g)) * u).astype(BF16), wsd_ref[...], preferred_element_type=F32)
        eye = lax.broadcasted_iota(I32, (tm, tm), 0) == lax.broadcasted_iota(I32, (tm, tm), 1)
        routed = jnp.zeros(x1.shape, F32)
        for k in range(TOP_K):
            wcol = jnp.sum(jnp.where(eye, wts_ref[k:k + 1, :], 0.0), axis=1, keepdims=True)
            routed = routed + _load_packed(gbuf.at[slot, k], tm) * wcol
        z = alpha * x1 + (routed + shared)
        mu = jnp.mean(z, axis=-1, keepdims=True)
        var = jnp.mean(jnp.square(z - mu), axis=-1, keepdims=True)
        y_ref[...] = (z - mu) * lax.rsqrt(var + EPS) * g2_ref[...] + b2_ref[...]

    @pl.when(i == 0)
    def _():
        issue(pos_ref, 0, False)

    for par in range(2):
        @pl.when((i & 1) == par)
        def _():
            wait(par)
            issue(posn_ref, 1 - par, True)
            combine(par)

            @pl.when(i == n - 1)
            def _():
                wait(1 - par)


def _final(pos, x1, wts, ys, wsg, wsu, wsd, g2, b2, *, tile, off, n, alpha):
    d = x1.shape[1]
    nt = n // tile
    kern = functools.partial(_final_kernel, alpha=alpha)
    return pl.pallas_call(
        kern,
        grid=(nt,),
        in_specs=[pl.BlockSpec((TOP_K, tile), lambda i: (0, i + off), memory_space=pltpu.SMEM),
                  pl.BlockSpec((TOP_K, tile), lambda i: (0, jnp.minimum(i + 1, nt - 1) + off),
                               memory_space=pltpu.SMEM),
                  pl.BlockSpec((tile, d), lambda i: (i + off, 0)),
                  pl.BlockSpec((TOP_K, tile), lambda i: (0, i + off)),
                  pl.BlockSpec(memory_space=pl.ANY),
                  _full(wsg.shape), _full(wsu.shape), _full(wsd.shape), _full(g2.shape), _full(b2.shape)],
        out_specs=pl.BlockSpec((tile, d), lambda i: (i, 0)),
        out_shape=jax.ShapeDtypeStruct((n, d), F32),
        scratch_shapes=[pltpu.VMEM((2, TOP_K, tile * PACK_ROWS, LANES), jnp.uint32),
                        pltpu.SemaphoreType.DMA((2,))],
        compiler_params=_cparams(("arbitrary",)),
        name="final",
    )(pos, pos, x1, wts, ys, wsg, wsu, wsd, g2, b2)


def _rope_tables(pos, qk_rope, nh):
    half = qk_rope // 2
    inv = ROPE_BASE ** (-jnp.arange(half, dtype=F32) / half)
    ang = pos.astype(F32)[:, None] * inv
    cos, sin = jnp.cos(ang), jnp.sin(ang)
    c = jnp.concatenate([cos, cos], axis=1)
    s = jnp.concatenate([-sin, sin], axis=1)
    rep = LANES // qk_rope
    rope_k = jnp.concatenate([jnp.tile(c, (1, rep)), jnp.tile(s, (1, rep))], axis=1)
    rope_q = jnp.concatenate([jnp.tile(c, (1, nh)), jnp.tile(s, (1, nh))], axis=1)
    return rope_k, rope_q


def _swap_halves(w, axis):
    a, b = jnp.split(w, 2, axis=axis)
    return jnp.concatenate([b, a], axis=axis)


def _block_diag_pairs(w):
    nh, r, c = w.shape
    z = jnp.zeros((nh // 2, r, c), w.dtype)
    top = jnp.concatenate([w[0::2], z], axis=2)
    bot = jnp.concatenate([z, w[1::2]], axis=2)
    return jnp.concatenate([top, bot], axis=1)


def kernel(x_prompt, x_sample, cache_mla_latent, cache_mla_krope, cache_diff_k, cache_diff_v, w_in, b_gate, g_q_norm, w_uq, w_uk, g_kv_norm, w_uv, w_o_mla, lambda_q1, lambda_k1, lambda_q2, lambda_k2, g_subln, w_o_diff, w_out, g_ln1, b_ln1, w_router, router_bias, w_exp_gate, w_exp_up, w_exp_down, w_sh_gate, w_sh_up, w_sh_down, g_ln2, b_ln2):
    depth = w_in.shape[0]
    assert depth == 1
    b, s, d = x_prompt.shape
    bs, ss, _ = x_sample.shape
    past = cache_mla_latent.shape[2]
    q_lora = g_q_norm.shape[1]
    kv_lora = g_kv_norm.shape[1]
    qk_rope = cache_mla_krope.shape[3]
    nh = w_uq.shape[2]
    qk_nope = w_uq.shape[3] - qk_rope
    mla_v = w_uv.shape[3]
    dnh, _, hd = cache_diff_k.shape[3:]
    diff_w = dnh * 2 * hd
    ne = w_router.shape[2]
    alpha = (2 * depth) ** 0.25
    lam_init = 0.8 - 0.6 * math.exp(-0.3 * 0)
    assert 2 * hd == LANES and kv_lora == LANES and 2 * qk_nope == LANES and LANES % qk_rope == 0

    wi = w_in[0]
    c0 = q_lora + kv_lora
    w_cq, w_ckv = wi[:, :q_lora], wi[:, q_lora:c0]
    w_kr = wi[:, c0:c0 + qk_rope]
    c1 = c0 + qk_rope
    w_d = wi[:, c1:c1 + 3 * diff_w]
    w_gates = wi[:, c1 + 3 * diff_w:]
    rep = LANES // qk_rope
    w1 = jnp.concatenate([w_cq, w_ckv, w_d, jnp.tile(w_kr, (1, rep)),
                          jnp.tile(_swap_halves(w_kr, 1), (1, rep))], axis=1).astype(BF16)
    uq = w_uq[0]
    wqn = uq[:, :, :qk_nope].reshape(q_lora, nh * qk_nope).astype(BF16)
    wqr = uq[:, :, qk_nope:].reshape(q_lora, nh * qk_rope).astype(BF16)
    wqs = _swap_halves(uq[:, :, qk_nope:], 2).reshape(q_lora, nh * qk_rope).astype(BF16)
    wuk = _block_diag_pairs(jnp.transpose(w_uk[0], (1, 2, 0))).astype(BF16)
    wuv = _block_diag_pairs(jnp.transpose(w_uv[0], (1, 0, 2))).astype(BF16)
    lam4 = jnp.concatenate([lambda_q1, lambda_k1, lambda_q2, lambda_k2], axis=0)
    merge_w = (w_gates.astype(BF16), b_gate, w_o_mla[0].astype(BF16), w_o_diff[0].astype(BF16),
               w_out[0].astype(BF16), g_ln1, b_ln1, jnp.transpose(w_router[0]).astype(BF16),
               jnp.transpose(router_bias))
    mla_scale = (qk_nope + qk_rope) ** -0.5
    dims = (q_lora, kv_lora, diff_w, qk_rope, hd ** -0.5 * LOG2E)

    n_p = b * s
    rope_k, rope_q = _rope_tables(jnp.arange(s, dtype=I32), qk_rope, nh)
    cq, lat, kr, kcat, dq, dk, dkb, dv, dvb = _proj(
        x_prompt.reshape(n_p, d), w1, g_q_norm, g_kv_norm, rope_k, tile=PROJ_TILE, dims=dims)
    r3 = lambda a: a.reshape(b, s, a.shape[1])
    oa, ob = _attn(r3(cq), rope_q, r3(kcat), wqn, wqr, wqs, wuk, wuv, r3(dq), r3(dkb), r3(dvb), lam4, g_subln,
                   qb=ATT_BLOCK, kb=ATT_BLOCK, nh=nh, dnh=dnh, hd=hd, qk_rope=qk_rope, past=0, n_valid=s,
                   scale=mla_scale, causal=True, lam_init=lam_init)

    n_s = bs * ss
    sk = past + ss
    skp = -(-sk // LANES) * LANES
    rope_ks, rope_qs = _rope_tables(past + jnp.arange(ss, dtype=I32), qk_rope, nh)
    cq_s, lat_s, kr_s, kcat_s, dq_s, dk_s, dkb_s, dv_s, dvb_s = _proj(
        x_sample.reshape(n_s, d), w1, g_q_norm, g_kv_norm, jnp.tile(rope_ks, (bs, 1)), tile=n_s, dims=dims)
    r3s = lambda a: a.reshape(bs, ss, a.shape[1])
    padk = lambda a: jnp.pad(a, ((0, 0), (0, skp - sk), (0, 0)))
    kcat_all = padk(jnp.concatenate(
        [jnp.concatenate([cache_mla_latent[0], jnp.tile(cache_mla_krope[0], (1, 1, rep))], axis=2).astype(BF16),
         r3s(kcat_s)], axis=1))
    dk_all = padk(jnp.concatenate([cache_diff_k[0].reshape(bs, past, diff_w).astype(BF16), r3s(dkb_s)], axis=1))
    dv_all = padk(jnp.concatenate([cache_diff_v[0].reshape(bs, past, diff_w).astype(BF16), r3s(dvb_s)], axis=1))
    oa_s, ob_s = _attn(r3s(cq_s), rope_qs, kcat_all, wqn, wqr, wqs, wuk, wuv, r3s(dq_s), dk_all, dv_all, lam4,
                       g_subln, qb=ss, kb=skp, nh=nh, dnh=dnh, hd=hd, qk_rope=qk_rope, past=past, n_valid=sk,
                       scale=mla_scale, causal=False, lam_init=lam_init)

    ntot = n_p + n_s
    x1, x1p, idx, wts = _merge(x_prompt.reshape(n_p, d), x_sample.reshape(n_s, d), oa.reshape(n_p, -1),
                          oa_s.reshape(n_s, -1), ob.reshape(n_p, -1), ob_s.reshape(n_s, -1), merge_w,
                          tile=TOK_TILE, alpha=alpha, ne=ne)
    nblocks = -(-(ntot * TOP_K) // EXP_BLOCK) + ne
    nbp = -(-nblocks // LANES) * LANES
    pos, be, bnv, bnext, nused = _plan(idx, ne=ne, blk=EXP_BLOCK, nbp=nbp, tile=TOK_TILE)
    xs = _scatter(pos, x1p, rows=nblocks * EXP_BLOCK, tile=TOK_TILE)
    ys = _ffn(be[0, :nblocks], bnv[0, :nblocks], bnext[0, :nblocks], nused[0, :1], xs, w_exp_gate[0], w_exp_up[0], w_exp_down[0],
              blk=EXP_BLOCK)
    fin_w = (w_sh_gate[0].astype(BF16), w_sh_up[0].astype(BF16), w_sh_down[0].astype(BF16), g_ln2, b_ln2)
    y_p = _final(pos, x1, wts, ys, *fin_w, tile=TOK_TILE, off=0, n=n_p, alpha=alpha)
    y_s = _final(pos, x1, wts, ys, *fin_w, tile=TOK_TILE, off=n_p // TOK_TILE, n=n_s, alpha=alpha)

    st = lambda a, bb, sq, tail: a.reshape((1, bb, sq) + tail)
    return (y_p.reshape(b, s, d), y_s.reshape(bs, ss, d),
            st(lat, b, s, (kv_lora,)), st(kr, b, s, (qk_rope,)),
            st(dk, b, s, (dnh, 2, hd)), st(dv, b, s, (dnh, 2 * hd)),
            st(lat_s, bs, ss, (kv_lora,)), st(kr_s, bs, ss, (qk_rope,)),
            st(dk_s, bs, ss, (dnh, 2, hd)), st(dv_s, bs, ss, (dnh, 2 * hd)))
```

```python
import functools
import math

import jax
import jax.numpy as jnp
from jax import lax
from jax.experimental import pallas as pl
from jax.experimental.pallas import tpu as pltpu

F32 = jnp.float32
BF16 = jnp.bfloat16
I32 = jnp.int32

CHUNK = 64
ROPE_BASE = 10000.0
EPS = 1e-6
N_GROUPS = 8
TOPK_GROUPS = 4
TOP_K = 8
ROUTED_SCALE = 2.5
NEG = -1e30

LANES = 128
TOK_TILE = 256
PROJ_TILE = 512
ATT_BLOCK = 512
SLAB_ROWS = 256
EXP_BLOCK = 512
FFN_SLOTS = 4
VMEM_LIMIT = 56 * 1024 * 1024


def _cparams(sem):
    return pltpu.CompilerParams(dimension_semantics=sem, vmem_limit_bytes=VMEM_LIMIT)


def _full(shape):
    nd = len(shape)
    return pl.BlockSpec(shape, lambda *_: (0,) * nd)


def _proj_kernel(x_ref, w_ref, gq_ref, gkv_ref, rope_ref,
                 cq_ref, lat_ref, kr_ref, kcat_ref, dq_ref, dk_ref, dkb_ref, dv_ref, dvb_ref,
                 *, q_lora, kv_lora, diff_w, qk_rope, dq_scale):
    x = x_ref[...].astype(BF16)
    h = jnp.dot(x, w_ref[...], preferred_element_type=F32)
    o = 0
    cq = h[:, o:o + q_lora]
    o += q_lora
    cq = cq * lax.rsqrt(jnp.mean(cq * cq, axis=-1, keepdims=True) + EPS) * gq_ref[...]
    cq_ref[...] = cq.astype(BF16)
    ckv = h[:, o:o + kv_lora]
    o += kv_lora
    lat = ckv * lax.rsqrt(jnp.mean(ckv * ckv, axis=-1, keepdims=True) + EPS) * gkv_ref[...]
    lat_ref[...] = lat
    dq_ref[...] = (h[:, o:o + diff_w] * dq_scale).astype(BF16)
    o += diff_w
    dk = h[:, o:o + diff_w]
    o += diff_w
    dk_ref[...] = dk
    dkb_ref[...] = dk.astype(BF16)
    dv = h[:, o:o + diff_w]
    o += diff_w
    dv_ref[...] = dv
    dvb_ref[...] = dv.astype(BF16)
    krr = h[:, o:o + LANES] * rope_ref[:, 0:LANES] + h[:, o + LANES:o + 2 * LANES] * rope_ref[:, LANES:2 * LANES]
    kr_ref[...] = krr[:, 0:qk_rope]
    kcat_ref[...] = jnp.concatenate([lat, krr], axis=1).astype(BF16)


def _proj(x, w1, gq, gkv, rope_k, *, tile, dims):
    n, d = x.shape
    q_lora, kv_lora, diff_w, qk_rope, dq_scale = dims
    wcols = w1.shape[1]
    nrope = rope_k.shape[0] // tile
    row = lambda c: pl.BlockSpec((tile, c), lambda i: (i, 0))
    outs = [
        jax.ShapeDtypeStruct((n, q_lora), BF16),
        jax.ShapeDtypeStruct((n, kv_lora), F32),
        jax.ShapeDtypeStruct((n, qk_rope), F32),
        jax.ShapeDtypeStruct((n, 2 * LANES), BF16),
        jax.ShapeDtypeStruct((n, diff_w), BF16),
        jax.ShapeDtypeStruct((n, diff_w), F32),
        jax.ShapeDtypeStruct((n, diff_w), BF16),
        jax.ShapeDtypeStruct((n, diff_w), F32),
        jax.ShapeDtypeStruct((n, diff_w), BF16),
    ]
    return pl.pallas_call(
        functools.partial(_proj_kernel, q_lora=q_lora, kv_lora=kv_lora, diff_w=diff_w, qk_rope=qk_rope,
                          dq_scale=dq_scale),
        grid=(n // tile,),
        in_specs=[row(d), _full((d, wcols)), _full((1, q_lora)), _full((1, kv_lora)),
                  pl.BlockSpec((tile, 2 * LANES), lambda i: (i % nrope, 0))],
        out_specs=[row(q_lora), row(kv_lora), row(qk_rope), row(2 * LANES), row(diff_w), row(diff_w),
                   row(diff_w), row(diff_w), row(diff_w)],
        out_shape=outs,
        compiler_params=_cparams(("arbitrary",)),
        name="proj",
    )(x, w1, gq, gkv, rope_k)


LOG2E = math.log2(math.e)


def _softmax_step(s2, v, m_scr, accl_scr, r0, rows, kb):
    m_prev = m_scr[r0:r0 + rows, :]
    m_next = jnp.maximum(m_prev, jnp.max(s2, axis=1, keepdims=True))
    p = jnp.exp2((s2 - jnp.concatenate([m_next] * (kb // LANES), axis=1)).astype(BF16))
    alpha = jnp.exp2(m_prev - m_next)
    m_scr[r0:r0 + rows, :] = m_next
    v1 = jnp.concatenate([v, jnp.ones(v.shape, BF16)], axis=1)
    pv = jnp.dot(p, v1, preferred_element_type=F32)
    accl_scr[r0:r0 + rows, :] = accl_scr[r0:r0 + rows, :] * jnp.concatenate([alpha, alpha], axis=1) + pv


def _visible(qb, kb, q0, k0, n_valid):
    qpos = q0 + lax.broadcasted_iota(I32, (qb, kb), 0)
    kpos = k0 + lax.broadcasted_iota(I32, (qb, kb), 1)
    sh = CHUNK.bit_length() - 1
    vis = (lax.shift_right_logical(kpos, sh) <= lax.shift_right_logical(qpos, sh)) & (kpos < n_valid)
    return vis, qpos, kpos


def _slab_rows(x, r0, rp, qb):
    if rp >= qb:
        return x[None]
    q0 = r0 % qb
    return x[q0:q0 + rp]


def _mask_slab(s2, vis, r0, rp, qb, w):
    if rp >= qb:
        return jnp.where(vis[None], s2.reshape(rp // qb, qb, w), NEG).reshape(rp, w)
    return jnp.where(_slab_rows(vis, r0, rp, qb), s2, NEG)


def _key_blocks(i, kb, causal, block):
    if not causal:
        block(0, kb, True)
        return

    def body(j, c):
        block(pl.multiple_of(j * kb, kb), kb, False)
        return c

    lax.fori_loop(0, i, body, 0)
    block(pl.multiple_of(i * kb, kb), kb, True)


def _attn_kernel(cq_ref, ropeq_ref, kcat_ref, wqn_ref, wqr_ref, wqs_ref, wuk_ref, wuv_ref,
                 dq_ref, dk_ref, dv_ref, lam_ref, gsub_ref, oa_ref, ob_ref,
                 qcat_scr, ma_scr, accla_scr, qq_scr, mb_scr, acclb_scr,
                 *, qb, kb, nh, dnh, hd, qk_rope, past, n_valid, scale, causal, lam_init):
    i = pl.program_id(1)
    rows_a = nh * qb
    c2 = scale * LOG2E
    cq = cq_ref[0]
    qn = jnp.dot(cq, wqn_ref[...], preferred_element_type=F32).astype(BF16)
    qr = jnp.dot(cq, wqr_ref[...], preferred_element_type=F32)
    qs = jnp.dot(cq, wqs_ref[...], preferred_element_type=F32)
    rw = nh * qk_rope
    rot = (qr * ropeq_ref[:, 0:rw] + qs * ropeq_ref[:, rw:2 * rw]) * c2
    lane = lax.broadcasted_iota(I32, (qb, LANES), 1)
    per = LANES // qk_rope
    for j in range(nh // 2):
        ql = jnp.dot(qn[:, LANES * j:LANES * (j + 1)], wuk_ref[j], preferred_element_type=F32) * c2
        for u in range(2):
            hh = 2 * j + u
            rblk = rot[:, LANES * (hh // per):LANES * (hh // per + 1)]
            lo = qk_rope * (hh % per)
            part2 = jnp.where((lane >= lo) & (lane < lo + qk_rope), rblk, 0.0)
            qcat_scr[hh * qb:(hh + 1) * qb, :] = jnp.concatenate(
                [ql[:, LANES * u:LANES * (u + 1)], part2], axis=1).astype(BF16)
    q = dq_ref[0]
    zero = jnp.zeros((), BF16)
    for h in range(dnh):
        qh = q[:, LANES * h:LANES * (h + 1)]
        qq_scr[(2 * h) * qb:(2 * h + 1) * qb, :] = jnp.where(lane < hd, qh, zero)
        qq_scr[(2 * h + 1) * qb:(2 * h + 2) * qb, :] = jnp.where(lane >= hd, qh, zero)
    ma_scr[...] = jnp.full(ma_scr.shape, NEG, F32)
    accla_scr[...] = jnp.zeros(accla_scr.shape, F32)
    mb_scr[...] = jnp.full(mb_scr.shape, NEG, F32)
    acclb_scr[...] = jnp.zeros(acclb_scr.shape, F32)

    def block(k0, w, masked):
        vis, qpos, kpos = _visible(qb, w, past + i * qb, k0, n_valid)
        dist = jnp.abs(qpos - kpos).astype(F32)
        kblk = kcat_ref[0, pl.ds(k0, w), :]
        work_a, work_b = [], []
        rp = min(rows_a, SLAB_ROWS)
        for sl in range(rows_a // rp):
            def slab_a(r0=sl * rp):
                s2 = lax.dot_general(qcat_scr[r0:r0 + rp, :], kblk, (((1,), (1,)), ((), ())),
                                     preferred_element_type=F32)
                if masked:
                    s2 = _mask_slab(s2, vis, r0, rp, qb, w)
                _softmax_step(s2, kblk[:, 0:LANES], ma_scr, accla_scr, r0, rp, w)
            work_a.append(slab_a)
        rpb = min(2 * qb, SLAB_ROWS)
        for h in range(dnh):
            slope2 = 2.0 ** (-8.0 * (h + 1) / dnh) * LOG2E
            for sl in range(2 * qb // rpb):
                def slab_b(h=h, slope2=slope2, r0=2 * h * qb + sl * rpb):
                    kh = dk_ref[0, pl.ds(k0, w), LANES * h:LANES * (h + 1)]
                    vh = dv_ref[0, pl.ds(k0, w), LANES * h:LANES * (h + 1)]
                    bias = slope2 * dist
                    s2 = lax.dot_general(qq_scr[r0:r0 + rpb, :], kh, (((1,), (1,)), ((), ())),
                                         preferred_element_type=F32)
                    if rpb >= qb:
                        s2 = (s2.reshape(rpb // qb, qb, w) - bias[None]).reshape(rpb, w)
                    else:
                        s2 = s2 - _slab_rows(bias, r0, rpb, qb)
                    if masked:
                        s2 = _mask_slab(s2, vis, r0, rpb, qb, w)
                    _softmax_step(s2, vh, mb_scr, acclb_scr, r0, rpb, w)
                work_b.append(slab_b)
        for j in range(max(len(work_a), len(work_b))):
            if j < len(work_a):
                work_a[j]()
            if j < len(work_b):
                work_b[j]()

    _key_blocks(i, kb, causal, block)

    o_lat = accla_scr[:, 0:LANES] / accla_scr[:, LANES:2 * LANES]
    for j in range(nh // 2):
        pair = jnp.concatenate([o_lat[(2 * j) * qb:(2 * j + 1) * qb, :],
                                o_lat[(2 * j + 1) * qb:(2 * j + 2) * qb, :]], axis=1).astype(BF16)
        oa_ref[0, :, LANES * j:LANES * (j + 1)] = jnp.dot(
            pair, wuv_ref[j], preferred_element_type=F32).astype(BF16)
    lam1 = jnp.sum(lam_ref[0:1, :] * lam_ref[1:2, :], axis=1, keepdims=True)
    lam2 = jnp.sum(lam_ref[2:3, :] * lam_ref[3:4, :], axis=1, keepdims=True)
    lam = jnp.exp(lam1) - jnp.exp(lam2) + lam_init
    o_all = acclb_scr[:, 0:LANES] / acclb_scr[:, LANES:2 * LANES]
    for h in range(dnh):
        o = o_all[(2 * h) * qb:(2 * h + 1) * qb, :] - lam * o_all[(2 * h + 1) * qb:(2 * h + 2) * qb, :]
        o = o * lax.rsqrt(jnp.mean(o * o, axis=-1, keepdims=True) + EPS) * gsub_ref[...]
        ob_ref[0, :, LANES * h:LANES * (h + 1)] = (o * (1.0 - lam_init)).astype(BF16)


def _attn(cq, rope_q, kcat, wqn, wqr, wqs, wuk, wuv, dq, dk, dv, lam4, gsub,
          *, qb, kb, nh, dnh, hd, qk_rope, past, n_valid, scale, causal, lam_init):
    b, sq, q_lora = cq.shape
    sk = kcat.shape[1]
    w = dq.shape[2]
    nq = sq // qb
    rows_a, rows_b = nh * qb, 2 * dnh * qb
    ov = wuv.shape[0] * wuv.shape[2]
    kern = functools.partial(_attn_kernel, qb=qb, kb=kb, nh=nh, dnh=dnh, hd=hd, qk_rope=qk_rope, past=past,
                             n_valid=n_valid, scale=scale, causal=causal, lam_init=lam_init)
    qblk = lambda c: pl.BlockSpec((1, qb, c), lambda bb, i: (bb, i, 0))
    kblk = lambda c: pl.BlockSpec((1, sk, c), lambda bb, i: (bb, 0, 0))
    return pl.pallas_call(
        kern,
        grid=(b, nq),
        in_specs=[qblk(q_lora), pl.BlockSpec((qb, rope_q.shape[1]), lambda bb, i: (i, 0)), kblk(kcat.shape[2]),
                  _full(wqn.shape), _full(wqr.shape), _full(wqs.shape), _full(wuk.shape), _full(wuv.shape),
                  qblk(w), kblk(w), kblk(w), _full(lam4.shape), _full(gsub.shape)],
        out_specs=[qblk(ov), qblk(w)],
        out_shape=[jax.ShapeDtypeStruct((b, sq, ov), BF16), jax.ShapeDtypeStruct((b, sq, w), BF16)],
        scratch_shapes=[pltpu.VMEM((rows_a, 2 * LANES), BF16), pltpu.VMEM((rows_a, LANES), F32),
                        pltpu.VMEM((rows_a, 2 * LANES), F32),
                        pltpu.VMEM((rows_b, LANES), BF16), pltpu.VMEM((rows_b, LANES), F32),
                        pltpu.VMEM((rows_b, 2 * LANES), F32)],
        compiler_params=_cparams(("arbitrary", "arbitrary")),
        name="attn",
    )(cq, rope_q, kcat, wqn, wqr, wqs, wuk, wuv, dq, dk, dv, lam4, gsub)


def _red2(x, fn):
    return fn(fn(x, axis=0, keepdims=True), axis=1, keepdims=True)


PACK_ROWS = 4


def _store_packed(ref, x):
    rows, d = x.shape
    bits = pltpu.bitcast(x.astype(BF16).astype(F32), jnp.uint32)
    packed = bits[:, 0:d // 2] | (bits[:, d // 2:d] >> 16)
    for j in range(PACK_ROWS):
        ref[pl.ds(j, rows, stride=PACK_ROWS), :] = packed[:, LANES * j:LANES * (j + 1)]


def _load_packed(ref, rows):
    words = [ref[pl.ds(j, rows, stride=PACK_ROWS), :] for j in range(PACK_ROWS)]
    hi = [pltpu.bitcast(w & jnp.uint32(0xFFFF0000), F32) for w in words]
    lo = [pltpu.bitcast(w << 16, F32) for w in words]
    return jnp.concatenate(hi + lo, axis=1)


def _merge_kernel(xp_ref, xs_ref, oap_ref, oas_ref, obp_ref, obs_ref, wg_ref, bg_ref, woa_ref, wob_ref,
                  wout_ref, g1_ref, b1_ref, wrt_ref, rb_ref, x1_ref, x1p_ref, idx_ref, wts_ref, *, alpha, d, ne, ntp):
    is_p = pl.program_id(0) < ntp
    x = jnp.where(is_p, xp_ref[...], xs_ref[...])
    oa = jnp.where(is_p, oap_ref[...], oas_ref[...])
    ob = jnp.where(is_p, obp_ref[...], obs_ref[...])
    gates = jnp.dot(x.astype(BF16), wg_ref[...], preferred_element_type=F32) + bg_ref[...]
    g = jax.nn.sigmoid(gates)
    br_a = jnp.dot(oa, woa_ref[...], preferred_element_type=F32)
    br_b = jnp.dot(ob, wob_ref[...], preferred_element_type=F32)
    mixin = g[:, 0:d] * br_a + g[:, d:2 * d] * br_b
    mix = jnp.dot(mixin.astype(BF16), wout_ref[...], preferred_element_type=F32)
    z = alpha * x + mix
    mu = jnp.mean(z, axis=-1, keepdims=True)
    var = jnp.mean(jnp.square(z - mu), axis=-1, keepdims=True)
    x1 = (z - mu) * lax.rsqrt(var + EPS) * g1_ref[...] + b1_ref[...]
    x1_ref[...] = x1
    _store_packed(x1p_ref, x1)

    tm = x.shape[0]
    gsz = ne // N_GROUPS
    logits = lax.dot_general(wrt_ref[...], x1.astype(BF16), (((1,), (1,)), ((), ())),
                             preferred_element_type=F32)
    scores = jax.nn.sigmoid(logits)
    choice = scores + rb_ref[...]
    c3 = choice.reshape(N_GROUPS, gsz, tm)
    s3 = scores.reshape(N_GROUPS, gsz, tm)
    io = lax.broadcasted_iota(I32, (N_GROUPS, gsz, tm), 1)
    gio = lax.broadcasted_iota(I32, (N_GROUPS, gsz, tm), 0)
    eio = gio * gsz + io
    ninf = -jnp.inf
    m1 = jnp.max(c3, axis=1, keepdims=True)
    i1 = jnp.min(jnp.where(c3 == m1, io, gsz), axis=1, keepdims=True)
    m2 = jnp.max(jnp.where(io == i1, ninf, c3), axis=1, keepdims=True)
    gs = m1 + m2
    gio1 = lax.broadcasted_iota(I32, (N_GROUPS, 1, tm), 0)
    gsel = jnp.zeros((N_GROUPS, 1, tm), F32)
    cur = gs
    for _ in range(TOPK_GROUPS):
        gm = jnp.max(cur, axis=0, keepdims=True)
        gi = jnp.min(jnp.where(cur == gm, gio1, N_GROUPS), axis=0, keepdims=True)
        hit = gio1 == gi
        gsel = jnp.where(hit, 1.0, gsel)
        cur = jnp.where(hit, ninf, cur)
    cur = jnp.where(gsel > 0.5, c3, ninf)
    idx_rows, w_rows = [], []
    for _ in range(TOP_K):
        m = _red2(cur, jnp.max)
        ik = _red2(jnp.where(cur == m, eio, ne), jnp.min)
        hit = eio == ik
        w_rows.append(_red2(jnp.where(hit, s3, 0.0), jnp.sum)[0])
        idx_rows.append(ik[0])
        cur = jnp.where(hit, ninf, cur)
    wsum = w_rows[0]
    for w in w_rows[1:]:
        wsum = wsum + w
    for k in range(TOP_K):
        idx_ref[k:k + 1, :] = idx_rows[k]
        wts_ref[k:k + 1, :] = w_rows[k] / wsum * ROUTED_SCALE


def _merge(xp, xs, oap, oas, obp, obs, weights, *, tile, alpha, ne):
    n_p, d = xp.shape
    n_s = xs.shape[0]
    ntp, nts = n_p // tile, n_s // tile
    ntot = n_p + n_s
    prow = lambda c: pl.BlockSpec((tile, c), lambda i: (jnp.minimum(i, ntp - 1), 0))
    srow = lambda c: pl.BlockSpec((tile, c), lambda i: (jnp.maximum(i - ntp, 0), 0))
    wa, wb = oap.shape[1], obp.shape[1]
    return pl.pallas_call(
        functools.partial(_merge_kernel, alpha=alpha, d=d, ne=ne, ntp=ntp),
        grid=(ntp + nts,),
        in_specs=[prow(d), srow(d), prow(wa), srow(wa), prow(wb), srow(wb)] + [_full(a.shape) for a in weights],
        out_specs=[pl.BlockSpec((tile, d), lambda i: (i, 0)),
                   pl.BlockSpec((tile * PACK_ROWS, LANES), lambda i: (i, 0)),
                   pl.BlockSpec((TOP_K, tile), lambda i: (0, i)),
                   pl.BlockSpec((TOP_K, tile), lambda i: (0, i))],
        out_shape=[jax.ShapeDtypeStruct((ntot, d), F32),
                   jax.ShapeDtypeStruct((ntot * PACK_ROWS, LANES), jnp.uint32),
                   jax.ShapeDtypeStruct((TOP_K, ntot), I32),
                   jax.ShapeDtypeStruct((TOP_K, ntot), F32)],
        compiler_params=_cparams(("arbitrary",)),
        name="merge",
    )(xp, xs, oap, oas, obp, obs, *weights)


def _plan_kernel(idx_ref, ltri_ref, su_ref, pos_ref, be_ref, bnv_ref, bnext_ref, nused_ref, cnt_scr, base_scr,
                 *, ne, blk, nbp, tile):
    nt = idx_ref.shape[1] // tile
    eio = lax.broadcasted_iota(I32, (ne, tile), 0)

    def tile_hits(i):
        c0 = pl.multiple_of(i * tile, tile)
        idx = idx_ref[:, pl.ds(c0, tile)]
        hits = [eio == idx[k:k + 1, :] for k in range(TOP_K)]
        mh = jnp.zeros((ne, tile), F32)
        for hk in hits:
            mh = jnp.where(hk, 1.0, mh)
        return c0, hits, mh

    cnt_scr[...] = jnp.zeros(cnt_scr.shape, F32)

    def count(i, c):
        _, _, mh = tile_hits(i)
        cnt_scr[...] = cnt_scr[...] + jnp.sum(mh, axis=1, keepdims=True)
        return c

    lax.fori_loop(0, nt, count, 0)

    cnt = cnt_scr[...]
    sh = blk.bit_length() - 1
    nb = lax.shift_right_logical(cnt.astype(I32) + (blk - 1), sh)
    nbf = nb.astype(F32)
    start = jnp.dot(ltri_ref[...], jnp.broadcast_to(nbf, (ne, LANES)).astype(BF16),
                    preferred_element_type=F32)[:, 0:1]
    end = start + nbf
    base_scr[...] = start * float(blk)
    bio = lax.broadcasted_iota(I32, (ne, nbp), 1).astype(F32)
    be = jnp.sum(jnp.where(end <= bio, 1.0, 0.0), axis=0, keepdims=True)
    be_ref[...] = jnp.minimum(be, float(ne - 1)).astype(I32)
    inside = (start <= bio) & (bio < end)
    left = jnp.clip(cnt - (bio - start) * float(blk), 0.0, float(blk))
    bnv_ref[...] = jnp.sum(jnp.where(inside, left, 0.0), axis=0, keepdims=True).astype(I32)
    bnext_ref[...] = jnp.sum(jnp.where(inside, end, 0.0), axis=0, keepdims=True).astype(I32)
    nused_ref[...] = jnp.broadcast_to(jnp.sum(nbf, axis=0, keepdims=True), (1, LANES)).astype(I32)

    def place(i, c):
        c0, hits, mh = tile_hits(i)
        before = jnp.dot(mh.astype(BF16), su_ref[...], preferred_element_type=F32)
        rank = base_scr[...] + before
        for k in range(TOP_K):
            row = jnp.sum(jnp.where(hits[k], rank, 0.0), axis=0, keepdims=True).astype(I32)
            pos_ref[k:k + 1, pl.ds(c0, tile)] = row * PACK_ROWS
        base_scr[...] = base_scr[...] + jnp.sum(mh, axis=1, keepdims=True)
        return c

    lax.fori_loop(0, nt, place, 0)


def _plan(idx, *, ne, blk, nbp, tile):
    ntot = idx.shape[1]
    ltri = jnp.tril(jnp.ones((ne, ne), F32), -1).astype(BF16)
    su = jnp.triu(jnp.ones((tile, tile), F32), 1).astype(BF16)
    kern = functools.partial(_plan_kernel, ne=ne, blk=blk, nbp=nbp, tile=tile)
    return pl.pallas_call(
        kern,
        grid=(1,),
        in_specs=[_full((TOP_K, ntot)), _full((ne, ne)), _full((tile, tile))],
        out_specs=[_full((TOP_K, ntot)), _full((1, nbp)), _full((1, nbp)), _full((1, nbp)), _full((1, LANES))],
        out_shape=[jax.ShapeDtypeStruct((TOP_K, ntot), I32), jax.ShapeDtypeStruct((1, nbp), I32),
                   jax.ShapeDtypeStruct((1, nbp), I32), jax.ShapeDtypeStruct((1, nbp), I32),
                   jax.ShapeDtypeStruct((1, LANES), I32)],
        scratch_shapes=[pltpu.VMEM((ne, 1), F32), pltpu.VMEM((ne, 1), F32)],
        compiler_params=_cparams(("arbitrary",)),
        name="plan",
    )(idx, ltri, su)


def _scatter_kernel(pos_ref, x_ref, xs_ref, sem):
    tm = pos_ref.shape[1]

    for t in range(tm):
        src = x_ref.at[pl.ds(t * PACK_ROWS, PACK_ROWS), :]
        for k in range(TOP_K):
            p = pl.multiple_of(pos_ref[k, t], PACK_ROWS)
            pltpu.make_async_copy(src, xs_ref.at[pl.ds(p, PACK_ROWS), :], sem).start(priority=k % 2)
    for _ in range(TOP_K):
        pltpu.make_async_copy(x_ref, xs_ref.at[pl.ds(0, tm * PACK_ROWS), :], sem).wait()


def _scatter(pos, x1p, *, rows, tile):
    ntot = pos.shape[1]
    return pl.pallas_call(
        _scatter_kernel,
        grid=(ntot // tile,),
        in_specs=[pl.BlockSpec((TOP_K, tile), lambda i: (0, i), memory_space=pltpu.SMEM),
                  pl.BlockSpec((tile * PACK_ROWS, LANES), lambda i: (i, 0))],
        out_specs=pl.BlockSpec(memory_space=pl.ANY),
        out_shape=jax.ShapeDtypeStruct((rows * PACK_ROWS, LANES), jnp.uint32),
        scratch_shapes=[pltpu.SemaphoreType.DMA],
        compiler_params=_cparams(("arbitrary",)),
        name="scatter",
    )(pos, x1p)


def _ffn_kernel(be_ref, bnv_ref, bnext_ref, nused_ref, xs_ref, wg_ref, wu_ref, wd_ref, ys_ref,
                xbuf, ybuf, wgf, wuf, wdf, wgb, wub, wdb, xsem, ysem, wsem, *, blk):
    nused = nused_ref[0]
    prow = blk * PACK_ROWS

    def x_copy(g, slot):
        return pltpu.make_async_copy(xs_ref.at[pl.ds(pl.multiple_of(g * prow, prow), prow), :],
                                     xbuf.at[slot], xsem.at[slot])

    def y_copy(g, slot):
        return pltpu.make_async_copy(ybuf.at[slot], ys_ref.at[pl.ds(pl.multiple_of(g * prow, prow), prow), :],
                                     ysem.at[slot])

    def w_copies(e, slot):
        return (pltpu.make_async_copy(wg_ref.at[e], wgf.at[slot], wsem.at[slot]),
                pltpu.make_async_copy(wu_ref.at[e], wuf.at[slot], wsem.at[slot]),
                pltpu.make_async_copy(wd_ref.at[e], wdf.at[slot], wsem.at[slot]))

    @pl.when(nused > 0)
    def _():
        for c in w_copies(be_ref[0], 0):
            c.start()

    for j in range(FFN_SLOTS - 1):
        @pl.when(j < nused)
        def _():
            x_copy(j, j).start()

    def body(g, ws):
        e = be_ref[g]
        first = (g == 0) | (be_ref[jnp.maximum(g - 1, 0)] != e)
        ws = jnp.where(first & (g > 0), 1 - ws, ws)

        @pl.when(first)
        def _():
            for c in w_copies(e, ws):
                c.wait()
            wgb[...] = wgf[ws].astype(BF16)
            wub[...] = wuf[ws].astype(BF16)
            wdb[...] = wdf[ws].astype(BF16)
            gn = bnext_ref[g]

            @pl.when(gn < nused)
            def _():
                for c in w_copies(be_ref[jnp.minimum(gn, nused - 1)], 1 - ws):
                    c.start()

        slot = g & (FFN_SLOTS - 1)
        ahead = g + (FFN_SLOTS - 1)

        @pl.when(ahead < nused)
        def _():
            x_copy(ahead, ahead & (FFN_SLOTS - 1)).start()

        x_copy(g, slot).wait()
        rid = lax.broadcasted_iota(I32, (blk, 1), 0)
        x = jnp.where(rid < bnv_ref[g], _load_packed(xbuf.at[slot], blk), 0.0).astype(BF16)
        gg = jnp.dot(x, wgb[...], preferred_element_type=F32)
        uu = jnp.dot(x, wub[...], preferred_element_type=F32)
        h = (gg * jax.nn.sigmoid(gg)) * uu
        y = jnp.dot(h.astype(BF16), wdb[...], preferred_element_type=F32)

        @pl.when(g >= FFN_SLOTS)
        def _():
            y_copy(g - FFN_SLOTS, slot).wait()

        _store_packed(ybuf.at[slot], y)
        y_copy(g, slot).start()
        return ws

    lax.fori_loop(0, nused, body, jnp.int32(0))

    for j in range(FFN_SLOTS, 0, -1):
        @pl.when(nused >= j)
        def _():
            y_copy(nused - j, (nused - j) & (FFN_SLOTS - 1)).wait()


def _ffn(be, bnv, bnext, nused, xs, wg, wu, wd, *, blk):
    d, de = wg.shape[1], wg.shape[2]
    prow = blk * PACK_ROWS
    any_spec = pl.BlockSpec(memory_space=pl.ANY)
    grid_spec = pltpu.PrefetchScalarGridSpec(
        num_scalar_prefetch=4,
        grid=(1,),
        in_specs=[any_spec, any_spec, any_spec, any_spec],
        out_specs=any_spec,
        scratch_shapes=[pltpu.VMEM((FFN_SLOTS, prow, LANES), jnp.uint32),
                        pltpu.VMEM((FFN_SLOTS, prow, LANES), jnp.uint32),
                        pltpu.VMEM((2, d, de), F32), pltpu.VMEM((2, d, de), F32), pltpu.VMEM((2, de, d), F32),
                        pltpu.VMEM((d, de), BF16), pltpu.VMEM((d, de), BF16), pltpu.VMEM((de, d), BF16),
                        pltpu.SemaphoreType.DMA((FFN_SLOTS,)), pltpu.SemaphoreType.DMA((FFN_SLOTS,)),
                        pltpu.SemaphoreType.DMA((2,))],
    )
    return pl.pallas_call(
        functools.partial(_ffn_kernel, blk=blk),
        grid_spec=grid_spec,
        out_shape=jax.ShapeDtypeStruct(xs.shape, jnp.uint32),
        compiler_params=_cparams(("arbitrary",)),
        name="ffn",
    )(be, bnv, bnext, nused, xs, wg, wu, wd)


def _final_kernel(pos_ref, posn_ref, x1_ref, wts_ref, ys_ref, wsg_ref, wsu_ref, wsd_ref, g2_ref, b2_ref, y_ref,
                  gbuf, sem, *, alpha):
    i = pl.program_id(0)
    n = pl.num_programs(0)
    tm = x1_ref.shape[0]

    def gather(p_ref, slot, t, dst):
        for k in range(TOP_K):
            p = pl.multiple_of(p_ref[k, t], PACK_ROWS)
            pltpu.make_async_copy(ys_ref.at[pl.ds(p, PACK_ROWS), :], gbuf.at[slot, k, pl.ds(dst, PACK_ROWS), :],
                                  sem.at[slot]).start(priority=k % 2)

    def issue(p_ref, slot, unrolled):
        if unrolled:
            for t in range(tm):
                gather(p_ref, slot, t, t * PACK_ROWS)
        else:
            def tok(t, c):
                gather(p_ref, slot, t, pl.multiple_of(t * PACK_ROWS, PACK_ROWS))
                return c
            lax.fori_loop(0, tm, tok, 0)

    def wait(slot):
        for k in range(TOP_K):
            pltpu.make_async_copy(ys_ref.at[pl.ds(0, tm * PACK_ROWS), :], gbuf.at[slot, k], sem.at[slot]).wait()

    def combine(slot):
        x1 = x1_ref[...]
        xb = x1.astype(BF16)
        g = jnp.dot(xb, wsg_ref[...], preferred_element_type=F32)
        u = jnp.dot(xb, wsu_ref[...], preferred_element_type=F32)
        shared = jnp.dot(((g * jax.nn.sigmoid(g)) * u).astype(BF16), wsd_ref[...], preferred_element_type=F32)
        eye = lax.broadcasted_iota(I32, (tm, tm), 0) == lax.broadcasted_iota(I32, (tm, tm), 1)
        routed = jnp.zeros(x1.shape, F32)
        for k in range(TOP_K):
            wcol = jnp.sum(jnp.where(eye, wts_ref[k:k + 1, :], 0.0), axis=1, keepdims=True)
            routed = routed + _load_packed(gbuf.at[slot, k], tm) * wcol
        z = alpha * x1 + (routed + shared)
        mu = jnp.mean(z, axis=-1, keepdims=True)
        var = jnp.mean(jnp.square(z - mu), axis=-1, keepdims=True)
        y_ref[...] = (z - mu) * lax.rsqrt(var + EPS) * g2_ref[...] + b2_ref[...]

    @pl.when(i == 0)
    def _():
        issue(pos_ref, 0, False)

    for par in range(2):
        @pl.when((i & 1) == par)
        def _():
            wait(par)
            issue(posn_ref, 1 - par, True)
            combine(par)

            @pl.when(i == n - 1)
            def _():
                wait(1 - par)


def _final(pos, x1, wts, ys, wsg, wsu, wsd, g2, b2, *, tile, off, n, alpha):
    d = x1.shape[1]
    nt = n // tile
    kern = functools.partial(_final_kernel, alpha=alpha)
    return pl.pallas_call(
        kern,
        grid=(nt,),
        in_specs=[pl.BlockSpec((TOP_K, tile), lambda i: (0, i + off), memory_space=pltpu.SMEM),
                  pl.BlockSpec((TOP_K, tile), lambda i: (0, jnp.minimum(i + 1, nt - 1) + off),
                               memory_space=pltpu.SMEM),
                  pl.BlockSpec((tile, d), lambda i: (i + off, 0)),
                  pl.BlockSpec((TOP_K, tile), lambda i: (0, i + off)),
                  pl.BlockSpec(memory_space=pl.ANY),
                  _full(wsg.shape), _full(wsu.shape), _full(wsd.shape), _full(g2.shape), _full(b2.shape)],
        out_specs=pl.BlockSpec((tile, d), lambda i: (i, 0)),
        out_shape=jax.ShapeDtypeStruct((n, d), F32),
        scratch_shapes=[pltpu.VMEM((2, TOP_K, tile * PACK_ROWS, LANES), jnp.uint32),
                        pltpu.SemaphoreType.DMA((2,))],
        compiler_params=_cparams(("arbitrary",)),
        name="final",
    )(pos, pos, x1, wts, ys, wsg, wsu, wsd, g2, b2)


def _rope_tables(pos, qk_rope, nh):
    half = qk_rope // 2
    inv = ROPE_BASE ** (-jnp.arange(half, dtype=F32) / half)
    ang = pos.astype(F32)[:, None] * inv
    cos, sin = jnp.cos(ang), jnp.sin(ang)
    c = jnp.concatenate([cos, cos], axis=1)
    s = jnp.concatenate([-sin, sin], axis=1)
    rep = LANES // qk_rope
    rope_k = jnp.concatenate([jnp.tile(c, (1, rep)), jnp.tile(s, (1, rep))], axis=1)
    rope_q = jnp.concatenate([jnp.tile(c, (1, nh)), jnp.tile(s, (1, nh))], axis=1)
    return rope_k, rope_q


def _swap_halves(w, axis):
    a, b = jnp.split(w, 2, axis=axis)
    return jnp.concatenate([b, a], axis=axis)


def _block_diag_pairs(w):
    nh, r, c = w.shape
    z = jnp.zeros((nh // 2, r, c), w.dtype)
    top = jnp.concatenate([w[0::2], z], axis=2)
    bot = jnp.concatenate([z, w[1::2]], axis=2)
    return jnp.concatenate([top, bot], axis=1)


def kernel(x_prompt, x_sample, cache_mla_latent, cache_mla_krope, cache_diff_k, cache_diff_v, w_in, b_gate, g_q_norm, w_uq, w_uk, g_kv_norm, w_uv, w_o_mla, lambda_q1, lambda_k1, lambda_q2, lambda_k2, g_subln, w_o_diff, w_out, g_ln1, b_ln1, w_router, router_bias, w_exp_gate, w_exp_up, w_exp_down, w_sh_gate, w_sh_up, w_sh_down, g_ln2, b_ln2):
    depth = w_in.shape[0]
    assert depth == 1
    b, s, d = x_prompt.shape
    bs, ss, _ = x_sample.shape
    past = cache_mla_latent.shape[2]
    q_lora = g_q_norm.shape[1]
    kv_lora = g_kv_norm.shape[1]
    qk_rope = cache_mla_krope.shape[3]
    nh = w_uq.shape[2]
    qk_nope = w_uq.shape[3] - qk_rope
    mla_v = w_uv.shape[3]
    dnh, _, hd = cache_diff_k.shape[3:]
    diff_w = dnh * 2 * hd
    ne = w_router.shape[2]
    alpha = (2 * depth) ** 0.25
    lam_init = 0.8 - 0.6 * math.exp(-0.3 * 0)
    assert 2 * hd == LANES and kv_lora == LANES and 2 * qk_nope == LANES and LANES % qk_rope == 0
    assert nh % 2 == 0 and ne % (8 * N_GROUPS) == 0 and d % (2 * LANES * PACK_ROWS) == 0
    assert s % ATT_BLOCK == 0 and (b * s) % PROJ_TILE == 0 and (b * s) % TOK_TILE == 0
    assert (bs * ss) % TOK_TILE == 0 and ss % 16 == 0

    wi = w_in[0]
    c0 = q_lora + kv_lora
    w_cq, w_ckv = wi[:, :q_lora], wi[:, q_lora:c0]
    w_kr = wi[:, c0:c0 + qk_rope]
    c1 = c0 + qk_rope
    w_d = wi[:, c1:c1 + 3 * diff_w]
    w_gates = wi[:, c1 + 3 * diff_w:]
    rep = LANES // qk_rope
    w1 = jnp.concatenate([w_cq, w_ckv, w_d, jnp.tile(w_kr, (1, rep)),
                          jnp.tile(_swap_halves(w_kr, 1), (1, rep))], axis=1).astype(BF16)
    uq = w_uq[0]
    wqn = uq[:, :, :qk_nope].reshape(q_lora, nh * qk_nope).astype(BF16)
    wqr = uq[:, :, qk_nope:].reshape(q_lora, nh * qk_rope).astype(BF16)
    wqs = _swap_halves(uq[:, :, qk_nope:], 2).reshape(q_lora, nh * qk_rope).astype(BF16)
    wuk = _block_diag_pairs(jnp.transpose(w_uk[0], (1, 2, 0))).astype(BF16)
    wuv = _block_diag_pairs(jnp.transpose(w_uv[0], (1, 0, 2))).astype(BF16)
    lam4 = jnp.concatenate([lambda_q1, lambda_k1, lambda_q2, lambda_k2], axis=0)
    merge_w = (w_gates.astype(BF16), b_gate, w_o_mla[0].astype(BF16), w_o_diff[0].astype(BF16),
               w_out[0].astype(BF16), g_ln1, b_ln1, jnp.transpose(w_router[0]).astype(BF16),
               jnp.transpose(router_bias))
    mla_scale = (qk_nope + qk_rope) ** -0.5
    dims = (q_lora, kv_lora, diff_w, qk_rope, hd ** -0.5 * LOG2E)

    n_p = b * s
    rope_k, rope_q = _rope_tables(jnp.arange(s, dtype=I32), qk_rope, nh)
    cq, lat, kr, kcat, dq, dk, dkb, dv, dvb = _proj(
        x_prompt.reshape(n_p, d), w1, g_q_norm, g_kv_norm, rope_k, tile=PROJ_TILE, dims=dims)
    r3 = lambda a: a.reshape(b, s, a.shape[1])
    oa, ob = _attn(r3(cq), rope_q, r3(kcat), wqn, wqr, wqs, wuk, wuv, r3(dq), r3(dkb), r3(dvb), lam4, g_subln,
                   qb=ATT_BLOCK, kb=ATT_BLOCK, nh=nh, dnh=dnh, hd=hd, qk_rope=qk_rope, past=0, n_valid=s,
                   scale=mla_scale, causal=True, lam_init=lam_init)

    n_s = bs * ss
    sk = past + ss
    skp = -(-sk // LANES) * LANES
    rope_ks, rope_qs = _rope_tables(past + jnp.arange(ss, dtype=I32), qk_rope, nh)
    cq_s, lat_s, kr_s, kcat_s, dq_s, dk_s, dkb_s, dv_s, dvb_s = _proj(
        x_sample.reshape(n_s, d), w1, g_q_norm, g_kv_norm, jnp.tile(rope_ks, (bs, 1)), tile=n_s, dims=dims)
    r3s = lambda a: a.reshape(bs, ss, a.shape[1])
    padk = lambda a: jnp.pad(a, ((0, 0), (0, skp - sk), (0, 0)))
    kcat_all = padk(jnp.concatenate(
        [jnp.concatenate([cache_mla_latent[0], jnp.tile(cache_mla_krope[0], (1, 1, rep))], axis=2).astype(BF16),
         r3s(kcat_s)], axis=1))
    dk_all = padk(jnp.concatenate([cache_diff_k[0].reshape(bs, past, diff_w).astype(BF16), r3s(dkb_s)], axis=1))
    dv_all = padk(jnp.concatenate([cache_diff_v[0].reshape(bs, past, diff_w).astype(BF16), r3s(dvb_s)], axis=1))
    oa_s, ob_s = _attn(r3s(cq_s), rope_qs, kcat_all, wqn, wqr, wqs, wuk, wuv, r3s(dq_s), dk_all, dv_all, lam4,
                       g_subln, qb=ss, kb=skp, nh=nh, dnh=dnh, hd=hd, qk_rope=qk_rope, past=past, n_valid=sk,
                       scale=mla_scale, causal=False, lam_init=lam_init)

    ntot = n_p + n_s
    x1, x1p, idx, wts = _merge(x_prompt.reshape(n_p, d), x_sample.reshape(n_s, d), oa.reshape(n_p, -1),
                          oa_s.reshape(n_s, -1), ob.reshape(n_p, -1), ob_s.reshape(n_s, -1), merge_w,
                          tile=TOK_TILE, alpha=alpha, ne=ne)
    nblocks = -(-(ntot * TOP_K) // EXP_BLOCK) + ne
    nbp = -(-nblocks // LANES) * LANES
    pos, be, bnv, bnext, nused = _plan(idx, ne=ne, blk=EXP_BLOCK, nbp=nbp, tile=TOK_TILE)
    xs = _scatter(pos, x1p, rows=nblocks * EXP_BLOCK, tile=TOK_TILE)
    ys = _ffn(be[0, :nblocks], bnv[0, :nblocks], bnext[0, :nblocks], nused[0, :1], xs, w_exp_gate[0], w_exp_up[0], w_exp_down[0],
              blk=EXP_BLOCK)
    fin_w = (w_sh_gate[0].astype(BF16), w_sh_up[0].astype(BF16), w_sh_down[0].astype(BF16), g_ln2, b_ln2)
    y_p = _final(pos, x1, wts, ys, *fin_w, tile=TOK_TILE, off=0, n=n_p, alpha=alpha)
    y_s = _final(pos, x1, wts, ys, *fin_w, tile=TOK_TILE, off=n_p // TOK_TILE, n=n_s, alpha=alpha)

    st = lambda a, bb, sq, tail: a.reshape((1, bb, sq) + tail)
    return (y_p.reshape(b, s, d), y_s.reshape(bs, ss, d),
            st(lat, b, s, (kv_lora,)), st(kr, b, s, (qk_rope,)),
            st(dk, b, s, (dnh, 2, hd)), st(dv, b, s, (dnh, 2 * hd)),
            st(lat_s, bs, ss, (kv_lora,)), st(kr_s, bs, ss, (qk_rope,)),
            st(dk_s, bs, ss, (dnh, 2, hd)), st(dv_s, bs, ss, (dnh, 2 * hd)))
```

```python
import functools
import math

import jax
import jax.numpy as jnp
from jax import lax
from jax.experimental import pallas as pl
from jax.experimental.pallas import tpu as pltpu

F32 = jnp.float32
BF16 = jnp.bfloat16
I32 = jnp.int32

CHUNK = 64
ROPE_BASE = 10000.0
EPS = 1e-6
N_GROUPS = 8
TOPK_GROUPS = 4
TOP_K = 8
ROUTED_SCALE = 2.5
NEG = -1e30

LANES = 128
TOK_TILE = 256
PROJ_TILE = 512
ATT_BLOCK = 512
SLAB_ROWS = 256
EXP_BLOCK = 512
FFN_SLOTS = 4
VMEM_LIMIT = 56 * 1024 * 1024


def _cparams(sem):
    return pltpu.CompilerParams(dimension_semantics=sem, vmem_limit_bytes=VMEM_LIMIT)


def _full(shape):
    nd = len(shape)
    return pl.BlockSpec(shape, lambda *_: (0,) * nd)


def _proj_kernel(x_ref, w_ref, gq_ref, gkv_ref, rope_ref,
                 cq_ref, lat_ref, kr_ref, kcat_ref, dq_ref, dk_ref, dkb_ref, dv_ref, dvb_ref,
                 *, q_lora, kv_lora, diff_w, qk_rope, dq_scale):
    x = x_ref[...].astype(BF16)
    h = jnp.dot(x, w_ref[...], preferred_element_type=F32)
    o = 0
    cq = h[:, o:o + q_lora]
    o += q_lora
    cq = cq * lax.rsqrt(jnp.mean(cq * cq, axis=-1, keepdims=True) + EPS) * gq_ref[...]
    cq_ref[...] = cq.astype(BF16)
    ckv = h[:, o:o + kv_lora]
    o += kv_lora
    lat = ckv * lax.rsqrt(jnp.mean(ckv * ckv, axis=-1, keepdims=True) + EPS) * gkv_ref[...]
    lat_ref[...] = lat
    dq_ref[...] = (h[:, o:o + diff_w] * dq_scale).astype(BF16)
    o += diff_w
    dk = h[:, o:o + diff_w]
    o += diff_w
    dk_ref[...] = dk
    dkb_ref[...] = dk.astype(BF16)
    dv = h[:, o:o + diff_w]
    o += diff_w
    dv_ref[...] = dv
    dvb_ref[...] = dv.astype(BF16)
    krr = h[:, o:o + LANES] * rope_ref[:, 0:LANES] + h[:, o + LANES:o + 2 * LANES] * rope_ref[:, LANES:2 * LANES]
    kr_ref[...] = krr[:, 0:qk_rope]
    kcat_ref[...] = jnp.concatenate([lat, krr], axis=1).astype(BF16)


def _proj(x, w1, gq, gkv, rope_k, *, tile, dims):
    n, d = x.shape
    q_lora, kv_lora, diff_w, qk_rope, dq_scale = dims
    wcols = w1.shape[1]
    nrope = rope_k.shape[0] // tile
    row = lambda c: pl.BlockSpec((tile, c), lambda i: (i, 0))
    outs = [
        jax.ShapeDtypeStruct((n, q_lora), BF16),
        jax.ShapeDtypeStruct((n, kv_lora), F32),
        jax.ShapeDtypeStruct((n, qk_rope), F32),
        jax.ShapeDtypeStruct((n, 2 * LANES), BF16),
        jax.ShapeDtypeStruct((n, diff_w), BF16),
        jax.ShapeDtypeStruct((n, diff_w), F32),
        jax.ShapeDtypeStruct((n, diff_w), BF16),
        jax.ShapeDtypeStruct((n, diff_w), F32),
        jax.ShapeDtypeStruct((n, diff_w), BF16),
    ]
    return pl.pallas_call(
        functools.partial(_proj_kernel, q_lora=q_lora, kv_lora=kv_lora, diff_w=diff_w, qk_rope=qk_rope,
                          dq_scale=dq_scale),
        grid=(n // tile,),
        in_specs=[row(d), _full((d, wcols)), _full((1, q_lora)), _full((1, kv_lora)),
                  pl.BlockSpec((tile, 2 * LANES), lambda i: (i % nrope, 0))],
        out_specs=[row(q_lora), row(kv_lora), row(qk_rope), row(2 * LANES), row(diff_w), row(diff_w),
                   row(diff_w), row(diff_w), row(diff_w)],
        out_shape=outs,
        compiler_params=_cparams(("arbitrary",)),
        name="proj",
    )(x, w1, gq, gkv, rope_k)


LOG2E = math.log2(math.e)


def _softmax_step(s2, v, m_scr, accl_scr, r0, rows, kb):
    m_prev = m_scr[r0:r0 + rows, :]
    m_next = jnp.maximum(m_prev, jnp.max(s2, axis=1, keepdims=True))
    p = jnp.exp2((s2 - jnp.concatenate([m_next] * (kb // LANES), axis=1)).astype(BF16))
    alpha = jnp.exp2(m_prev - m_next)
    m_scr[r0:r0 + rows, :] = m_next
    v1 = jnp.concatenate([v, jnp.ones(v.shape, BF16)], axis=1)
    pv = jnp.dot(p, v1, preferred_element_type=F32)
    accl_scr[r0:r0 + rows, :] = accl_scr[r0:r0 + rows, :] * jnp.concatenate([alpha, alpha], axis=1) + pv


def _visible(qb, kb, q0, k0, n_valid):
    qpos = q0 + lax.broadcasted_iota(I32, (qb, kb), 0)
    kpos = k0 + lax.broadcasted_iota(I32, (qb, kb), 1)
    sh = CHUNK.bit_length() - 1
    vis = (lax.shift_right_logical(kpos, sh) <= lax.shift_right_logical(qpos, sh)) & (kpos < n_valid)
    return vis, qpos, kpos


def _slab_rows(x, r0, rp, qb):
    if rp >= qb:
        return x[None]
    q0 = r0 % qb
    return x[q0:q0 + rp]


def _mask_slab(s2, vis, r0, rp, qb, w):
    if rp >= qb:
        return jnp.where(vis[None], s2.reshape(rp // qb, qb, w), NEG).reshape(rp, w)
    return jnp.where(_slab_rows(vis, r0, rp, qb), s2, NEG)


def _key_blocks(i, kb, causal, block):
    if not causal:
        block(0, kb, True)
        return

    def body(j, c):
        block(pl.multiple_of(j * kb, kb), kb, False)
        return c

    lax.fori_loop(0, i, body, 0)
    block(pl.multiple_of(i * kb, kb), kb, True)


def _attn_kernel(cq_ref, ropeq_ref, kcat_ref, wqn_ref, wqr_ref, wqs_ref, wuk_ref, wuv_ref,
                 dq_ref, dk_ref, dv_ref, lam_ref, gsub_ref, oa_ref, ob_ref,
                 qcat_scr, ma_scr, accla_scr, qq_scr, mb_scr, acclb_scr,
                 *, qb, kb, nh, dnh, hd, qk_rope, past, n_valid, scale, causal, lam_init):
    i = pl.program_id(1)
    rows_a = nh * qb
    c2 = scale * LOG2E
    cq = cq_ref[0]
    qn = jnp.dot(cq, wqn_ref[...], preferred_element_type=F32).astype(BF16)
    qr = jnp.dot(cq, wqr_ref[...], preferred_element_type=F32)
    qs = jnp.dot(cq, wqs_ref[...], preferred_element_type=F32)
    rw = nh * qk_rope
    rot = (qr * ropeq_ref[:, 0:rw] + qs * ropeq_ref[:, rw:2 * rw]) * c2
    lane = lax.broadcasted_iota(I32, (qb, LANES), 1)
    per = LANES // qk_rope
    for j in range(nh // 2):
        ql = jnp.dot(qn[:, LANES * j:LANES * (j + 1)], wuk_ref[j], preferred_element_type=F32) * c2
        for u in range(2):
            hh = 2 * j + u
            rblk = rot[:, LANES * (hh // per):LANES * (hh // per + 1)]
            lo = qk_rope * (hh % per)
            part2 = jnp.where((lane >= lo) & (lane < lo + qk_rope), rblk, 0.0)
            qcat_scr[hh * qb:(hh + 1) * qb, :] = jnp.concatenate(
                [ql[:, LANES * u:LANES * (u + 1)], part2], axis=1).astype(BF16)
    q = dq_ref[0]
    zero = jnp.zeros((), BF16)
    for h in range(dnh):
        qh = q[:, LANES * h:LANES * (h + 1)]
        qq_scr[(2 * h) * qb:(2 * h + 1) * qb, :] = jnp.where(lane < hd, qh, zero)
        qq_scr[(2 * h + 1) * qb:(2 * h + 2) * qb, :] = jnp.where(lane >= hd, qh, zero)
    ma_scr[...] = jnp.full(ma_scr.shape, NEG, F32)
    accla_scr[...] = jnp.zeros(accla_scr.shape, F32)
    mb_scr[...] = jnp.full(mb_scr.shape, NEG, F32)
    acclb_scr[...] = jnp.zeros(acclb_scr.shape, F32)

    def block(k0, w, masked):
        vis, qpos, kpos = _visible(qb, w, past + i * qb, k0, n_valid)
        dist = jnp.abs(qpos - kpos).astype(F32)
        kblk = kcat_ref[0, pl.ds(k0, w), :]
        work_a, work_b = [], []
        rp = min(rows_a, SLAB_ROWS)
        for sl in range(rows_a // rp):
            def slab_a(r0=sl * rp):
                s2 = lax.dot_general(qcat_scr[r0:r0 + rp, :], kblk, (((1,), (1,)), ((), ())),
                                     preferred_element_type=F32)
                if masked:
                    s2 = _mask_slab(s2, vis, r0, rp, qb, w)
                _softmax_step(s2, kblk[:, 0:LANES], ma_scr, accla_scr, r0, rp, w)
            work_a.append(slab_a)
        rpb = min(2 * qb, SLAB_ROWS)
        for h in range(dnh):
            slope2 = 2.0 ** (-8.0 * (h + 1) / dnh) * LOG2E
            for sl in range(2 * qb // rpb):
                def slab_b(h=h, slope2=slope2, r0=2 * h * qb + sl * rpb):
                    kh = dk_ref[0, pl.ds(k0, w), LANES * h:LANES * (h + 1)]
                    vh = dv_ref[0, pl.ds(k0, w), LANES * h:LANES * (h + 1)]
                    bias = slope2 * dist
                    s2 = lax.dot_general(qq_scr[r0:r0 + rpb, :], kh, (((1,), (1,)), ((), ())),
                                         preferred_element_type=F32)
                    if rpb >= qb:
                        s2 = (s2.reshape(rpb // qb, qb, w) - bias[None]).reshape(rpb, w)
                    else:
                        s2 = s2 - _slab_rows(bias, r0, rpb, qb)
                    if masked:
                        s2 = _mask_slab(s2, vis, r0, rpb, qb, w)
                    _softmax_step(s2, vh, mb_scr, acclb_scr, r0, rpb, w)
                work_b.append(slab_b)
        for j in range(max(len(work_a), len(work_b))):
            if j < len(work_a):
                work_a[j]()
            if j < len(work_b):
                work_b[j]()

    _key_blocks(i, kb, causal, block)

    o_lat = accla_scr[:, 0:LANES] / accla_scr[:, LANES:2 * LANES]
    for j in range(nh // 2):
        pair = jnp.concatenate([o_lat[(2 * j) * qb:(2 * j + 1) * qb, :],
                                o_lat[(2 * j + 1) * qb:(2 * j + 2) * qb, :]], axis=1).astype(BF16)
        oa_ref[0, :, LANES * j:LANES * (j + 1)] = jnp.dot(
            pair, wuv_ref[j], preferred_element_type=F32).astype(BF16)
    lam1 = jnp.sum(lam_ref[0:1, :] * lam_ref[1:2, :], axis=1, keepdims=True)
    lam2 = jnp.sum(lam_ref[2:3, :] * lam_ref[3:4, :], axis=1, keepdims=True)
    lam = jnp.exp(lam1) - jnp.exp(lam2) + lam_init
    o_all = acclb_scr[:, 0:LANES] / acclb_scr[:, LANES:2 * LANES]
    for h in range(dnh):
        o = o_all[(2 * h) * qb:(2 * h + 1) * qb, :] - lam * o_all[(2 * h + 1) * qb:(2 * h + 2) * qb, :]
        o = o * lax.rsqrt(jnp.mean(o * o, axis=-1, keepdims=True) + EPS) * gsub_ref[...]
        ob_ref[0, :, LANES * h:LANES * (h + 1)] = (o * (1.0 - lam_init)).astype(BF16)


def _attn(cq, rope_q, kcat, wqn, wqr, wqs, wuk, wuv, dq, dk, dv, lam4, gsub,
          *, qb, kb, nh, dnh, hd, qk_rope, past, n_valid, scale, causal, lam_init):
    b, sq, q_lora = cq.shape
    sk = kcat.shape[1]
    w = dq.shape[2]
    nq = sq // qb
    rows_a, rows_b = nh * qb, 2 * dnh * qb
    ov = wuv.shape[0] * wuv.shape[2]
    kern = functools.partial(_attn_kernel, qb=qb, kb=kb, nh=nh, dnh=dnh, hd=hd, qk_rope=qk_rope, past=past,
                             n_valid=n_valid, scale=scale, causal=causal, lam_init=lam_init)
    qblk = lambda c: pl.BlockSpec((1, qb, c), lambda bb, i: (bb, i, 0))
    kblk = lambda c: pl.BlockSpec((1, sk, c), lambda bb, i: (bb, 0, 0))
    return pl.pallas_call(
        kern,
        grid=(b, nq),
        in_specs=[qblk(q_lora), pl.BlockSpec((qb, rope_q.shape[1]), lambda bb, i: (i, 0)), kblk(kcat.shape[2]),
                  _full(wqn.shape), _full(wqr.shape), _full(wqs.shape), _full(wuk.shape), _full(wuv.shape),
                  qblk(w), kblk(w), kblk(w), _full(lam4.shape), _full(gsub.shape)],
        out_specs=[qblk(ov), qblk(w)],
        out_shape=[jax.ShapeDtypeStruct((b, sq, ov), BF16), jax.ShapeDtypeStruct((b, sq, w), BF16)],
        scratch_shapes=[pltpu.VMEM((rows_a, 2 * LANES), BF16), pltpu.VMEM((rows_a, LANES), F32),
                        pltpu.VMEM((rows_a, 2 * LANES), F32),
                        pltpu.VMEM((rows_b, LANES), BF16), pltpu.VMEM((rows_b, LANES), F32),
                        pltpu.VMEM((rows_b, 2 * LANES), F32)],
        compiler_params=_cparams(("arbitrary", "arbitrary")),
        name="attn",
    )(cq, rope_q, kcat, wqn, wqr, wqs, wuk, wuv, dq, dk, dv, lam4, gsub)


def _red2(x, fn):
    return fn(fn(x, axis=0, keepdims=True), axis=1, keepdims=True)


PACK_ROWS = 4


def _store_packed(ref, x):
    rows, d = x.shape
    bits = pltpu.bitcast(x.astype(BF16).astype(F32), jnp.uint32)
    packed = bits[:, 0:d // 2] | (bits[:, d // 2:d] >> 16)
    for j in range(PACK_ROWS):
        ref[pl.ds(j, rows, stride=PACK_ROWS), :] = packed[:, LANES * j:LANES * (j + 1)]


def _load_packed(ref, rows):
    words = [ref[pl.ds(j, rows, stride=PACK_ROWS), :] for j in range(PACK_ROWS)]
    hi = [pltpu.bitcast(w & jnp.uint32(0xFFFF0000), F32) for w in words]
    lo = [pltpu.bitcast(w << 16, F32) for w in words]
    return jnp.concatenate(hi + lo, axis=1)


def _merge_kernel(xp_ref, xs_ref, oap_ref, oas_ref, obp_ref, obs_ref, wg_ref, bg_ref, woa_ref, wob_ref,
                  wout_ref, g1_ref, b1_ref, wrt_ref, rb_ref, x1_ref, x1p_ref, idx_ref, wts_ref, *, alpha, d, ne, ntp):
    is_p = pl.program_id(0) < ntp
    x = jnp.where(is_p, xp_ref[...], xs_ref[...])
    oa = jnp.where(is_p, oap_ref[...], oas_ref[...])
    ob = jnp.where(is_p, obp_ref[...], obs_ref[...])
    gates = jnp.dot(x.astype(BF16), wg_ref[...], preferred_element_type=F32) + bg_ref[...]
    g = jax.nn.sigmoid(gates)
    br_a = jnp.dot(oa, woa_ref[...], preferred_element_type=F32)
    br_b = jnp.dot(ob, wob_ref[...], preferred_element_type=F32)
    mixin = g[:, 0:d] * br_a + g[:, d:2 * d] * br_b
    mix = jnp.dot(mixin.astype(BF16), wout_ref[...], preferred_element_type=F32)
    z = alpha * x + mix
    mu = jnp.mean(z, axis=-1, keepdims=True)
    var = jnp.mean(jnp.square(z - mu), axis=-1, keepdims=True)
    x1 = (z - mu) * lax.rsqrt(var + EPS) * g1_ref[...] + b1_ref[...]
    x1_ref[...] = x1
    _store_packed(x1p_ref, x1)

    tm = x.shape[0]
    gsz = ne // N_GROUPS
    logits = lax.dot_general(wrt_ref[...], x1.astype(BF16), (((1,), (1,)), ((), ())),
                             preferred_element_type=F32)
    scores = jax.nn.sigmoid(logits)
    choice = scores + rb_ref[...]
    c3 = choice.reshape(N_GROUPS, gsz, tm)
    s3 = scores.reshape(N_GROUPS, gsz, tm)
    io = lax.broadcasted_iota(I32, (N_GROUPS, gsz, tm), 1)
    gio = lax.broadcasted_iota(I32, (N_GROUPS, gsz, tm), 0)
    eio = gio * gsz + io
    ninf = -jnp.inf
    m1 = jnp.max(c3, axis=1, keepdims=True)
    i1 = jnp.min(jnp.where(c3 == m1, io, gsz), axis=1, keepdims=True)
    m2 = jnp.max(jnp.where(io == i1, ninf, c3), axis=1, keepdims=True)
    gs = m1 + m2
    gio1 = lax.broadcasted_iota(I32, (N_GROUPS, 1, tm), 0)
    gsel = jnp.zeros((N_GROUPS, 1, tm), F32)
    cur = gs
    for _ in range(TOPK_GROUPS):
        gm = jnp.max(cur, axis=0, keepdims=True)
        gi = jnp.min(jnp.where(cur == gm, gio1, N_GROUPS), axis=0, keepdims=True)
        hit = gio1 == gi
        gsel = jnp.where(hit, 1.0, gsel)
        cur = jnp.where(hit, ninf, cur)
    cur = jnp.where(gsel > 0.5, c3, ninf)
    idx_rows, w_rows = [], []
    for _ in range(TOP_K):
        m = _red2(cur, jnp.max)
        ik = _red2(jnp.where(cur == m, eio, ne), jnp.min)
        hit = eio == ik
        w_rows.append(_red2(jnp.where(hit, s3, 0.0), jnp.sum)[0])
        idx_rows.append(ik[0])
        cur = jnp.where(hit, ninf, cur)
    wsum = w_rows[0]
    for w in w_rows[1:]:
        wsum = wsum + w
    for k in range(TOP_K):
        idx_ref[k:k + 1, :] = idx_rows[k]
        wts_ref[k:k + 1, :] = w_rows[k] / wsum * ROUTED_SCALE


def _merge(xp, xs, oap, oas, obp, obs, weights, *, tile, alpha, ne):
    n_p, d = xp.shape
    n_s = xs.shape[0]
    ntp, nts = n_p // tile, n_s // tile
    ntot = n_p + n_s
    prow = lambda c: pl.BlockSpec((tile, c), lambda i: (jnp.minimum(i, ntp - 1), 0))
    srow = lambda c: pl.BlockSpec((tile, c), lambda i: (jnp.maximum(i - ntp, 0), 0))
    wa, wb = oap.shape[1], obp.shape[1]
    return pl.pallas_call(
        functools.partial(_merge_kernel, alpha=alpha, d=d, ne=ne, ntp=ntp),
        grid=(ntp + nts,),
        in_specs=[prow(d), srow(d), prow(wa), srow(wa), prow(wb), srow(wb)] + [_full(a.shape) for a in weights],
        out_specs=[pl.BlockSpec((tile, d), lambda i: (i, 0)),
                   pl.BlockSpec((tile * PACK_ROWS, LANES), lambda i: (i, 0)),
                   pl.BlockSpec((TOP_K, tile), lambda i: (0, i)),
                   pl.BlockSpec((TOP_K, tile), lambda i: (0, i))],
        out_shape=[jax.ShapeDtypeStruct((ntot, d), F32),
                   jax.ShapeDtypeStruct((ntot * PACK_ROWS, LANES), jnp.uint32),
                   jax.ShapeDtypeStruct((TOP_K, ntot), I32),
                   jax.ShapeDtypeStruct((TOP_K, ntot), F32)],
        compiler_params=_cparams(("arbitrary",)),
        name="merge",
    )(xp, xs, oap, oas, obp, obs, *weights)


def _plan_kernel(idx_ref, ltri_ref, su_ref, pos_ref, be_ref, bnv_ref, bnext_ref, nused_ref, cnt_scr, base_scr,
                 *, ne, blk, nbp, tile):
    nt = idx_ref.shape[1] // tile
    eio = lax.broadcasted_iota(I32, (ne, tile), 0)

    def tile_hits(i):
        c0 = pl.multiple_of(i * tile, tile)
        idx = idx_ref[:, pl.ds(c0, tile)]
        hits = [eio == idx[k:k + 1, :] for k in range(TOP_K)]
        mh = jnp.zeros((ne, tile), F32)
        for hk in hits:
            mh = jnp.where(hk, 1.0, mh)
        return c0, hits, mh

    cnt_scr[...] = jnp.zeros(cnt_scr.shape, F32)

    def count(i, c):
        _, _, mh = tile_hits(i)
        cnt_scr[...] = cnt_scr[...] + jnp.sum(mh, axis=1, keepdims=True)
        return c

    lax.fori_loop(0, nt, count, 0)

    cnt = cnt_scr[...]
    sh = blk.bit_length() - 1
    nb = lax.shift_right_logical(cnt.astype(I32) + (blk - 1), sh)
    nbf = nb.astype(F32)
    start = jnp.dot(ltri_ref[...], jnp.broadcast_to(nbf, (ne, LANES)).astype(BF16),
                    preferred_element_type=F32)[:, 0:1]
    end = start + nbf
    base_scr[...] = start * float(blk)
    bio = lax.broadcasted_iota(I32, (ne, nbp), 1).astype(F32)
    be = jnp.sum(jnp.where(end <= bio, 1.0, 0.0), axis=0, keepdims=True)
    be_ref[...] = jnp.minimum(be, float(ne - 1)).astype(I32)
    inside = (start <= bio) & (bio < end)
    left = jnp.clip(cnt - (bio - start) * float(blk), 0.0, float(blk))
    bnv_ref[...] = jnp.sum(jnp.where(inside, left, 0.0), axis=0, keepdims=True).astype(I32)
    bnext_ref[...] = jnp.sum(jnp.where(inside, end, 0.0), axis=0, keepdims=True).astype(I32)
    nused_ref[...] = jnp.broadcast_to(jnp.sum(nbf, axis=0, keepdims=True), (1, LANES)).astype(I32)

    def place(i, c):
        c0, hits, mh = tile_hits(i)
        before = jnp.dot(mh.astype(BF16), su_ref[...], preferred_element_type=F32)
        rank = base_scr[...] + before
        for k in range(TOP_K):
            row = jnp.sum(jnp.where(hits[k], rank, 0.0), axis=0, keepdims=True).astype(I32)
            pos_ref[k:k + 1, pl.ds(c0, tile)] = row * PACK_ROWS
        base_scr[...] = base_scr[...] + jnp.sum(mh, axis=1, keepdims=True)
        return c

    lax.fori_loop(0, nt, place, 0)


def _plan(idx, *, ne, blk, nbp, tile):
    ntot = idx.shape[1]
    ltri = jnp.tril(jnp.ones((ne, ne), F32), -1).astype(BF16)
    su = jnp.triu(jnp.ones((tile, tile), F32), 1).astype(BF16)
    kern = functools.partial(_plan_kernel, ne=ne, blk=blk, nbp=nbp, tile=tile)
    return pl.pallas_call(
        kern,
        grid=(1,),
        in_specs=[_full((TOP_K, ntot)), _full((ne, ne)), _full((tile, tile))],
        out_specs=[_full((TOP_K, ntot)), _full((1, nbp)), _full((1, nbp)), _full((1, nbp)), _full((1, LANES))],
        out_shape=[jax.ShapeDtypeStruct((TOP_K, ntot), I32), jax.ShapeDtypeStruct((1, nbp), I32),
                   jax.ShapeDtypeStruct((1, nbp), I32), jax.ShapeDtypeStruct((1, nbp), I32),
                   jax.ShapeDtypeStruct((1, LANES), I32)],
        scratch_shapes=[pltpu.VMEM((ne, 1), F32), pltpu.VMEM((ne, 1), F32)],
        compiler_params=_cparams(("arbitrary",)),
        name="plan",
    )(idx, ltri, su)


def _scatter_kernel(pos_ref, x_ref, xs_ref, sem):
    tm = pos_ref.shape[1]

    for t in range(tm):
        src = x_ref.at[pl.ds(t * PACK_ROWS, PACK_ROWS), :]
        for k in range(TOP_K):
            p = pl.multiple_of(pos_ref[k, t], PACK_ROWS)
            pltpu.make_async_copy(src, xs_ref.at[pl.ds(p, PACK_ROWS), :], sem).start(priority=k % 2)
    for _ in range(TOP_K):
        pltpu.make_async_copy(x_ref, xs_ref.at[pl.ds(0, tm * PACK_ROWS), :], sem).wait()


def _scatter(pos, x1p, *, rows, tile):
    ntot = pos.shape[1]
    return pl.pallas_call(
        _scatter_kernel,
        grid=(ntot // tile,),
        in_specs=[pl.BlockSpec((TOP_K, tile), lambda i: (0, i), memory_space=pltpu.SMEM),
                  pl.BlockSpec((tile * PACK_ROWS, LANES), lambda i: (i, 0))],
        out_specs=pl.BlockSpec(memory_space=pl.ANY),
        out_shape=jax.ShapeDtypeStruct((rows * PACK_ROWS, LANES), jnp.uint32),
        scratch_shapes=[pltpu.SemaphoreType.DMA],
        compiler_params=_cparams(("arbitrary",)),
        name="scatter",
    )(pos, x1p)


def _ffn_kernel(be_ref, bnv_ref, bnext_ref, nused_ref, xs_ref, wg_ref, wu_ref, wd_ref, ys_ref,
                xbuf, ybuf, wgf, wuf, wdf, wgb, wub, wdb, xsem, ysem, wsem, *, blk):
    nused = nused_ref[0]
    prow = blk * PACK_ROWS

    def x_copy(g, slot):
        return pltpu.make_async_copy(xs_ref.at[pl.ds(pl.multiple_of(g * prow, prow), prow), :],
                                     xbuf.at[slot], xsem.at[slot])

    def y_copy(g, slot):
        return pltpu.make_async_copy(ybuf.at[slot], ys_ref.at[pl.ds(pl.multiple_of(g * prow, prow), prow), :],
                                     ysem.at[slot])

    def w_copies(e, slot):
        return (pltpu.make_async_copy(wg_ref.at[e], wgf.at[slot], wsem.at[slot]),
                pltpu.make_async_copy(wu_ref.at[e], wuf.at[slot], wsem.at[slot]),
                pltpu.make_async_copy(wd_ref.at[e], wdf.at[slot], wsem.at[slot]))

    @pl.when(nused > 0)
    def _():
        for c in w_copies(be_ref[0], 0):
            c.start()

    for j in range(FFN_SLOTS - 1):
        @pl.when(j < nused)
        def _():
            x_copy(j, j).start()

    def body(g, ws):
        e = be_ref[g]
        first = (g == 0) | (be_ref[jnp.maximum(g - 1, 0)] != e)
        ws = jnp.where(first & (g > 0), 1 - ws, ws)

        @pl.when(first)
        def _():
            for c in w_copies(e, ws):
                c.wait()
            wgb[...] = wgf[ws].astype(BF16)
            wub[...] = wuf[ws].astype(BF16)
            wdb[...] = wdf[ws].astype(BF16)
            gn = bnext_ref[g]

            @pl.when(gn < nused)
            def _():
                for c in w_copies(be_ref[jnp.minimum(gn, nused - 1)], 1 - ws):
                    c.start()

        slot = g & (FFN_SLOTS - 1)
        ahead = g + (FFN_SLOTS - 1)

        @pl.when(ahead < nused)
        def _():
            x_copy(ahead, ahead & (FFN_SLOTS - 1)).start()

        x_copy(g, slot).wait()
        rid = lax.broadcasted_iota(I32, (blk, 1), 0)
        x = jnp.where(rid < bnv_ref[g], _load_packed(xbuf.at[slot], blk), 0.0).astype(BF16)
        gg = jnp.dot(x, wgb[...], preferred_element_type=F32)
        uu = jnp.dot(x, wub[...], preferred_element_type=F32)
        h = (gg * jax.nn.sigmoid(gg)) * uu
        y = jnp.dot(h.astype(BF16), wdb[...], preferred_element_type=F32)

        @pl.when(g >= FFN_SLOTS)
        def _():
            y_copy(g - FFN_SLOTS, slot).wait()

        _store_packed(ybuf.at[slot], y)
        y_copy(g, slot).start()
        return ws

    lax.fori_loop(0, nused, body, jnp.int32(0))

    for j in range(FFN_SLOTS, 0, -1):
        @pl.when(nused >= j)
        def _():
            y_copy(nused - j, (nused - j) & (FFN_SLOTS - 1)).wait()


def _ffn(be, bnv, bnext, nused, xs, wg, wu, wd, *, blk):
    d, de = wg.shape[1], wg.shape[2]
    prow = blk * PACK_ROWS
    any_spec = pl.BlockSpec(memory_space=pl.ANY)
    grid_spec = pltpu.PrefetchScalarGridSpec(
        num_scalar_prefetch=4,
        grid=(1,),
        in_specs=[any_spec, any_spec, any_spec, any_spec],
        out_specs=any_spec,
        scratch_shapes=[pltpu.VMEM((FFN_SLOTS, prow, LANES), jnp.uint32),
                        pltpu.VMEM((FFN_SLOTS, prow, LANES), jnp.uint32),
                        pltpu.VMEM((2, d, de), F32), pltpu.VMEM((2, d, de), F32), pltpu.VMEM((2, de, d), F32),
                        pltpu.VMEM((d, de), BF16), pltpu.VMEM((d, de), BF16), pltpu.VMEM((de, d), BF16),
                        pltpu.SemaphoreType.DMA((FFN_SLOTS,)), pltpu.SemaphoreType.DMA((FFN_SLOTS,)),
                        pltpu.SemaphoreType.DMA((2,))],
    )
    return pl.pallas_call(
        functools.partial(_ffn_kernel, blk=blk),
        grid_spec=grid_spec,
        out_shape=jax.ShapeDtypeStruct(xs.shape, jnp.uint32),
        compiler_params=_cparams(("arbitrary",)),
        name="ffn",
    )(be, bnv, bnext, nused, xs, wg, wu, wd)


def _final_kernel(pos_ref, posn_ref, x1_ref, wts_ref, ys_ref, wsg_ref, wsu_ref, wsd_ref, g2_ref, b2_ref, y_ref,
                  gbuf, sem, *, alpha):
    i = pl.program_id(0)
    n = pl.num_programs(0)
    tm = x1_ref.shape[0]

    def gather(p_ref, slot, t, dst):
        for k in range(TOP_K):
            p = pl.multiple_of(p_ref[k, t], PACK_ROWS)
            pltpu.make_async_copy(ys_ref.at[pl.ds(p, PACK_ROWS), :], gbuf.at[slot, k, pl.ds(dst, PACK_ROWS), :],
                                  sem.at[slot]).start(priority=k % 2)

    def issue(p_ref, slot, unrolled):
        if unrolled:
            for t in range(tm):
                gather(p_ref, slot, t, t * PACK_ROWS)
        else:
            def tok(t, c):
                gather(p_ref, slot, t, pl.multiple_of(t * PACK_ROWS, PACK_ROWS))
                return c
            lax.fori_loop(0, tm, tok, 0)

    def wait(slot):
        for k in range(TOP_K):
            pltpu.make_async_copy(ys_ref.at[pl.ds(0, tm * PACK_ROWS), :], gbuf.at[slot, k], sem.at[slot]).wait()

    def combine(slot):
        x1 = x1_ref[...]
        xb = x1.astype(BF16)
        g = jnp.dot(xb, wsg_ref[...], preferred_element_type=F32)
        u = jnp.dot(xb, wsu_ref[...], preferred_element_type=F32)
        shared = jnp.dot(((g * jax.nn.sigmoid(g)) * u).astype(BF16), wsd_ref[...], preferred_element_type=F32)
        eye = lax.broadcasted_iota(I32, (tm, tm), 0) == lax.broadcasted_iota(I32, (tm, tm), 1)
        routed = jnp.zeros(x1.shape, F32)
        for k in range(TOP_K):
            wcol = jnp.sum(jnp.where(eye, wts_ref[k:k + 1, :], 0.0), axis=1, keepdims=True)
            routed = routed + _load_packed(gbuf.at[slot, k], tm) * wcol
        z = alpha * x1 + (routed + shared)
        mu = jnp.mean(z, axis=-1, keepdims=True)
        var = jnp.mean(jnp.square(z - mu), axis=-1, keepdims=True)
        y_ref[...] = (z - mu) * lax.rsqrt(var + EPS) * g2_ref[...] + b2_ref[...]

    @pl.when(i == 0)
    def _():
        issue(pos_ref, 0, False)

    for par in range(2):
        @pl.when((i & 1) == par)
        def _():
            issue(posn_ref, 1 - par, True)
            wait(par)
            combine(par)

            @pl.when(i == n - 1)
            def _():
                wait(1 - par)


def _final(pos, x1, wts, ys, wsg, wsu, wsd, g2, b2, *, tile, off, n, alpha):
    d = x1.shape[1]
    nt = n // tile
    kern = functools.partial(_final_kernel, alpha=alpha)
    return pl.pallas_call(
        kern,
        grid=(nt,),
        in_specs=[pl.BlockSpec((TOP_K, tile), lambda i: (0, i + off), memory_space=pltpu.SMEM),
                  pl.BlockSpec((TOP_K, tile), lambda i: (0, jnp.minimum(i + 1, nt - 1) + off),
                               memory_space=pltpu.SMEM),
                  pl.BlockSpec((tile, d), lambda i: (i + off, 0)),
                  pl.BlockSpec((TOP_K, tile), lambda i: (0, i + off)),
                  pl.BlockSpec(memory_space=pl.ANY),
                  _full(wsg.shape), _full(wsu.shape), _full(wsd.shape), _full(g2.shape), _full(b2.shape)],
        out_specs=pl.BlockSpec((tile, d), lambda i: (i, 0)),
        out_shape=jax.ShapeDtypeStruct((n, d), F32),
        scratch_shapes=[pltpu.VMEM((2, TOP_K, tile * PACK_ROWS, LANES), jnp.uint32),
                        pltpu.SemaphoreType.DMA((2,))],
        compiler_params=_cparams(("arbitrary",)),
        name="final",
    )(pos, pos, x1, wts, ys, wsg, wsu, wsd, g2, b2)


def _rope_tables(pos, qk_rope, nh):
    half = qk_rope // 2
    inv = ROPE_BASE ** (-jnp.arange(half, dtype=F32) / half)
    ang = pos.astype(F32)[:, None] * inv
    cos, sin = jnp.cos(ang), jnp.sin(ang)
    c = jnp.concatenate([cos, cos], axis=1)
    s = jnp.concatenate([-sin, sin], axis=1)
    rep = LANES // qk_rope
    rope_k = jnp.concatenate([jnp.tile(c, (1, rep)), jnp.tile(s, (1, rep))], axis=1)
    rope_q = jnp.concatenate([jnp.tile(c, (1, nh)), jnp.tile(s, (1, nh))], axis=1)
    return rope_k, rope_q


def _swap_halves(w, axis):
    a, b = jnp.split(w, 2, axis=axis)
    return jnp.concatenate([b, a], axis=axis)


def _block_diag_pairs(w):
    nh, r, c = w.shape
    z = jnp.zeros((nh // 2, r, c), w.dtype)
    top = jnp.concatenate([w[0::2], z], axis=2)
    bot = jnp.concatenate([z, w[1::2]], axis=2)
    return jnp.concatenate([top, bot], axis=1)


def kernel(x_prompt, x_sample, cache_mla_latent, cache_mla_krope, cache_diff_k, cache_diff_v, w_in, b_gate, g_q_norm, w_uq, w_uk, g_kv_norm, w_uv, w_o_mla, lambda_q1, lambda_k1, lambda_q2, lambda_k2, g_subln, w_o_diff, w_out, g_ln1, b_ln1, w_router, router_bias, w_exp_gate, w_exp_up, w_exp_down, w_sh_gate, w_sh_up, w_sh_down, g_ln2, b_ln2):
    depth = w_in.shape[0]
    assert depth == 1
    b, s, d = x_prompt.shape
    bs, ss, _ = x_sample.shape
    past = cache_mla_latent.shape[2]
    q_lora = g_q_norm.shape[1]
    kv_lora = g_kv_norm.shape[1]
    qk_rope = cache_mla_krope.shape[3]
    nh = w_uq.shape[2]
    qk_nope = w_uq.shape[3] - qk_rope
    mla_v = w_uv.shape[3]
    dnh, _, hd = cache_diff_k.shape[3:]
    diff_w = dnh * 2 * hd
    ne = w_router.shape[2]
    alpha = (2 * depth) ** 0.25
    lam_init = 0.8 - 0.6 * math.exp(-0.3 * 0)
    assert 2 * hd == LANES and kv_lora == LANES and 2 * qk_nope == LANES and LANES % qk_rope == 0
    assert nh % 2 == 0 and ne % (8 * N_GROUPS) == 0 and d % (2 * LANES * PACK_ROWS) == 0
    assert s % ATT_BLOCK == 0 and (b * s) % PROJ_TILE == 0 and (b * s) % TOK_TILE == 0
    assert (bs * ss) % TOK_TILE == 0 and ss % 16 == 0

    wi = w_in[0]
    c0 = q_lora + kv_lora
    w_cq, w_ckv = wi[:, :q_lora], wi[:, q_lora:c0]
    w_kr = wi[:, c0:c0 + qk_rope]
    c1 = c0 + qk_rope
    w_d = wi[:, c1:c1 + 3 * diff_w]
    w_gates = wi[:, c1 + 3 * diff_w:]
    rep = LANES // qk_rope
    w1 = jnp.concatenate([w_cq, w_ckv, w_d, jnp.tile(w_kr, (1, rep)),
                          jnp.tile(_swap_halves(w_kr, 1), (1, rep))], axis=1).astype(BF16)
    uq = w_uq[0]
    wqn = uq[:, :, :qk_nope].reshape(q_lora, nh * qk_nope).astype(BF16)
    wqr = uq[:, :, qk_nope:].reshape(q_lora, nh * qk_rope).astype(BF16)
    wqs = _swap_halves(uq[:, :, qk_nope:], 2).reshape(q_lora, nh * qk_rope).astype(BF16)
    wuk = _block_diag_pairs(jnp.transpose(w_uk[0], (1, 2, 0))).astype(BF16)
    wuv = _block_diag_pairs(jnp.transpose(w_uv[0], (1, 0, 2))).astype(BF16)
    lam4 = jnp.concatenate([lambda_q1, lambda_k1, lambda_q2, lambda_k2], axis=0)
    merge_w = (w_gates.astype(BF16), b_gate, w_o_mla[0].astype(BF16), w_o_diff[0].astype(BF16),
               w_out[0].astype(BF16), g_ln1, b_ln1, jnp.transpose(w_router[0]).astype(BF16),
               jnp.transpose(router_bias))
    mla_scale = (qk_nope + qk_rope) ** -0.5
    dims = (q_lora, kv_lora, diff_w, qk_rope, hd ** -0.5 * LOG2E)

    n_p = b * s
    rope_k, rope_q = _rope_tables(jnp.arange(s, dtype=I32), qk_rope, nh)
    cq, lat, kr, kcat, dq, dk, dkb, dv, dvb = _proj(
        x_prompt.reshape(n_p, d), w1, g_q_norm, g_kv_norm, rope_k, tile=PROJ_TILE, dims=dims)
    r3 = lambda a: a.reshape(b, s, a.shape[1])
    oa, ob = _attn(r3(cq), rope_q, r3(kcat), wqn, wqr, wqs, wuk, wuv, r3(dq), r3(dkb), r3(dvb), lam4, g_subln,
                   qb=ATT_BLOCK, kb=ATT_BLOCK, nh=nh, dnh=dnh, hd=hd, qk_rope=qk_rope, past=0, n_valid=s,
                   scale=mla_scale, causal=True, lam_init=lam_init)

    n_s = bs * ss
    sk = past + ss
    skp = -(-sk // LANES) * LANES
    rope_ks, rope_qs = _rope_tables(past + jnp.arange(ss, dtype=I32), qk_rope, nh)
    cq_s, lat_s, kr_s, kcat_s, dq_s, dk_s, dkb_s, dv_s, dvb_s = _proj(
        x_sample.reshape(n_s, d), w1, g_q_norm, g_kv_norm, jnp.tile(rope_ks, (bs, 1)), tile=n_s, dims=dims)
    r3s = lambda a: a.reshape(bs, ss, a.shape[1])
    padk = lambda a: jnp.pad(a, ((0, 0), (0, skp - sk), (0, 0)))
    kcat_all = padk(jnp.concatenate(
        [jnp.concatenate([cache_mla_latent[0], jnp.tile(cache_mla_krope[0], (1, 1, rep))], axis=2).astype(BF16),
         r3s(kcat_s)], axis=1))
    dk_all = padk(jnp.concatenate([cache_diff_k[0].reshape(bs, past, diff_w).astype(BF16), r3s(dkb_s)], axis=1))
    dv_all = padk(jnp.concatenate([cache_diff_v[0].reshape(bs, past, diff_w).astype(BF16), r3s(dvb_s)], axis=1))
    oa_s, ob_s = _attn(r3s(cq_s), rope_qs, kcat_all, wqn, wqr, wqs, wuk, wuv, r3s(dq_s), dk_all, dv_all, lam4,
                       g_subln, qb=ss, kb=skp, nh=nh, dnh=dnh, hd=hd, qk_rope=qk_rope, past=past, n_valid=sk,
                       scale=mla_scale, causal=False, lam_init=lam_init)

    ntot = n_p + n_s
    x1, x1p, idx, wts = _merge(x_prompt.reshape(n_p, d), x_sample.reshape(n_s, d), oa.reshape(n_p, -1),
                          oa_s.reshape(n_s, -1), ob.reshape(n_p, -1), ob_s.reshape(n_s, -1), merge_w,
                          tile=TOK_TILE, alpha=alpha, ne=ne)
    nblocks = -(-(ntot * TOP_K) // EXP_BLOCK) + ne
    nbp = -(-nblocks // LANES) * LANES
    pos, be, bnv, bnext, nused = _plan(idx, ne=ne, blk=EXP_BLOCK, nbp=nbp, tile=TOK_TILE)
    xs = _scatter(pos, x1p, rows=nblocks * EXP_BLOCK, tile=TOK_TILE)
    ys = _ffn(be[0, :nblocks], bnv[0, :nblocks], bnext[0, :nblocks], nused[0, :1], xs, w_exp_gate[0], w_exp_up[0], w_exp_down[0],
              blk=EXP_BLOCK)
    fin_w = (w_sh_gate[0].astype(BF16), w_sh_up[0].astype(BF16), w_sh_down[0].astype(BF16), g_ln2, b_ln2)
    y_p = _final(pos, x1, wts, ys, *fin_w, tile=TOK_TILE, off=0, n=n_p, alpha=alpha)
    y_s = _final(pos, x1, wts, ys, *fin_w, tile=TOK_TILE, off=n_p // TOK_TILE, n=n_s, alpha=alpha)

    st = lambda a, bb, sq, tail: a.reshape((1, bb, sq) + tail)
    return (y_p.reshape(b, s, d), y_s.reshape(bs, ss, d),
            st(lat, b, s, (kv_lora,)), st(kr, b, s, (qk_rope,)),
            st(dk, b, s, (dnh, 2, hd)), st(dv, b, s, (dnh, 2 * hd)),
            st(lat_s, bs, ss, (kv_lora,)), st(kr_s, bs, ss, (qk_rope,)),
            st(dk_s, bs, ss, (dnh, 2, hd)), st(dv_s, bs, ss, (dnh, 2 * hd)))
```
